```python
import math, functools
import jax, jax.numpy as jnp
from jax import lax
import numpy as np

D_MODEL = 1024
BATCH = 2
SEQ = 8192
DEPTH = 4
DEC_BATCH = 32
DEC_SEQ = 4
PAST_LEN = 8192
PAGE_SIZE = 128

CONV_GROUPS = 4
CONV_W = 256
CONV_K = 3
ATT_HEADS = 8
QK_DIM = 32
V_DIM = 64
ATT_W = ATT_HEADS * V_DIM
CHUNK_GROUPS = 4
CHUNK_GW = 64
CHUNK_W = CHUNK_GROUPS * CHUNK_GW
CHUNK = 128
MIX_W = CONV_W + ATT_W + CHUNK_W
O_CH = 0
O_CB = O_CH + CONV_W
O_CC = O_CB + CONV_W
O_Q = O_CC + CONV_W
O_K = O_Q + ATT_HEADS * 2 * QK_DIM
O_V = O_K + ATT_HEADS * 2 * QK_DIM
O_U = O_V + ATT_W
O_SV = O_U + CHUNK_W
IN_W = O_SV + CHUNK_W
N_EXPERTS = 32
TOP_K = 4
D_FF = 1024
SWIGLU_ALPHA = 1.702
SWIGLU_LIMIT = 7.0
MOE_BLOCK = 128
Q_BLOCK = 128
ROPE_THETA = 10000.0
LN_EPS = 1e-5
DN_ALPHA = (2 * DEPTH) ** 0.25
DN_BETA = (8 * DEPTH) ** -0.25

kernel_name = "hymba_conv_diffattn_chunkmlp_moe_step"


def layer_norm(x, g, b):
    xf = x.astype(jnp.float32)
    mu = xf.mean(-1, keepdims=True)
    var = jnp.square(xf - mu).mean(-1, keepdims=True)
    return ((xf - mu) * lax.rsqrt(var + LN_EPS) * g + b).astype(x.dtype)


def rms_norm(x, g):
    xf = x.astype(jnp.float32)
    return (xf * lax.rsqrt(jnp.mean(xf * xf, -1, keepdims=True) + LN_EPS) * g).astype(x.dtype)


def rope(x, pos):
    half = QK_DIM // 2
    inv = ROPE_THETA ** (-jnp.arange(half, dtype=jnp.float32) / half)
    ang = pos.astype(jnp.float32)[:, None] * inv[None, :]
    cos = jnp.cos(ang)[:, None, None, :]
    sin = jnp.sin(ang)[:, None, None, :]
    xf = x.astype(jnp.float32)
    x1, x2 = xf[..., :half], xf[..., half:]
    return jnp.concatenate([x1 * cos - x2 * sin, x2 * cos + x1 * sin], -1).astype(x.dtype)


def short_conv(gate_b, gate_c, conv_in, prev, w):
    z = gate_c * conv_in
    zp = jnp.concatenate([prev.astype(z.dtype), z], axis=1)
    t = z.shape[1]
    y = w[0] * zp[:, 0:t]
    for j in range(1, CONV_K):
        y = y + w[j] * zp[:, j:j + t]
    return gate_b * y, zp[:, -(CONV_K - 1):]


def chunk_spatial_gate(u, sv, ln_g, ln_b, ws, bs):
    b_, t_ = u.shape[:2]
    L = min(t_, CHUNK)
    n = t_ // L
    vn = layer_norm(sv.reshape(b_, t_, CHUNK_GROUPS, CHUNK_GW),
                    ln_g.reshape(CHUNK_GROUPS, CHUNK_GW), ln_b.reshape(CHUNK_GROUPS, CHUNK_GW))
    vc = vn.reshape(b_, n, L, CHUNK_GROUPS, CHUNK_GW)
    wm = jnp.tril(ws[:, :L, :L])
    mixed = jnp.einsum('gts,bnsgc->bntgc', wm, vc) + bs[:, :L].T[:, :, None]
    out = u * mixed.reshape(b_, t_, CHUNK_W)
    return out, vn.reshape(b_, t_, CHUNK_W)[:, t_ - L:]


def diff_combine(s, lam):
    p = jax.nn.softmax(s, axis=-1)
    return p[:, :, 0] - lam * p[:, :, 1]


def attn_prompt(q, k, v, lam):
    b_, s_ = q.shape[:2]
    nblk = s_ // Q_BLOCK
    scale = QK_DIM ** -0.5
    qb = jnp.moveaxis(q.reshape(b_, nblk, Q_BLOCK, ATT_HEADS, 2, QK_DIM), 1, 0)
    kpos = jnp.arange(s_)

    def one_block(args):
        qblk, i = args
        s = jnp.einsum('bqhmd,bkhmd->bhmqk', qblk, k).astype(jnp.float32) * scale
        qpos = i * Q_BLOCK + jnp.arange(Q_BLOCK)
        s = jnp.where(kpos[None, :] <= qpos[:, None], s, -jnp.inf)
        a = diff_combine(s, lam).astype(v.dtype)
        return jnp.einsum('bhqk,bkhe->bqhe', a, v)

    o = lax.map(one_block, (qb, jnp.arange(nblk)))
    return jnp.moveaxis(o, 0, 1).reshape(b_, s_, ATT_HEADS, V_DIM)


def attn_sample(q, k, v, lam, k_past, v_past):
    scale = QK_DIM ** -0.5
    t_ = q.shape[1]
    p_len = k_past.shape[1]
    s_past = jnp.einsum('bqhmd,bkhmd->bhmqk', q, k_past).astype(jnp.float32) * scale
    s_new = jnp.einsum('bqhmd,bkhmd->bhmqk', q, k).astype(jnp.float32) * scale
    s_new = jnp.where(jnp.tril(jnp.ones((t_, t_), bool)), s_new, -jnp.inf)
    a = diff_combine(jnp.concatenate([s_past, s_new], -1), lam).astype(v.dtype)
    return (jnp.einsum('bhqk,bkhe->bqhe', a[..., :p_len], v_past)
            + jnp.einsum('bhqk,bkhe->bqhe', a[..., p_len:], v))


def moe_ffn(x2d, w_r, b_r, wg, bg, wu, bu, wd, bd):
    t = x2d.shape[0]
    logits = (x2d @ w_r + b_r).astype(jnp.float32)
    top_v, top_i = lax.top_k(logits, TOP_K)
    gates = jax.nn.softmax(top_v, axis=-1)
    n_assign = t * TOP_K
    flat_e = top_i.reshape(-1).astype(jnp.int32)
    order = jnp.argsort(flat_e).astype(jnp.int32)
    sorted_e = flat_e[order]
    counts = jnp.bincount(flat_e, length=N_EXPERTS).astype(jnp.int32)
    starts = jnp.cumsum(counts) - counts
    pcounts = ((counts + MOE_BLOCK - 1) // MOE_BLOCK) * MOE_BLOCK
    pends = jnp.cumsum(pcounts)
    pstarts = pends - pcounts
    dest = pstarts[sorted_e] + jnp.arange(n_assign, dtype=jnp.int32) - starts[sorted_e]
    n_blocks = -(-n_assign // MOE_BLOCK) + N_EXPERTS
    n_rows = n_blocks * MOE_BLOCK
    row_assign = jnp.full((n_rows,), n_assign, jnp.int32).at[dest].set(order)
    row_tok = row_assign // TOP_K
    block_e = jnp.minimum(jnp.searchsorted(pends, jnp.arange(n_blocks, dtype=jnp.int32) * MOE_BLOCK,
                                           side='right'), N_EXPERTS - 1)
    x_pad = jnp.concatenate([x2d, jnp.zeros((1, x2d.shape[1]), x2d.dtype)], 0)
    xb = x_pad[row_tok].reshape(n_blocks, MOE_BLOCK, x2d.shape[1])

    def expert_block(args):
        xblk, e = args
        g = jnp.minimum(xblk @ wg[e] + bg[e], SWIGLU_LIMIT)
        u = jnp.clip(xblk @ wu[e] + bu[e], -SWIGLU_LIMIT, SWIGLU_LIMIT)
        hmid = (u + 1.0) * (g * jax.nn.sigmoid(g * SWIGLU_ALPHA))
        return hmid @ wd[e] + bd[e]

    yb = lax.map(expert_block, (xb, block_e)).reshape(n_rows, x2d.shape[1])
    w_rows = jnp.concatenate([gates.reshape(-1), jnp.zeros((1,), jnp.float32)])[row_assign]
    y = jax.ops.segment_sum(yb * w_rows[:, None].astype(yb.dtype), row_tok, num_segments=t + 1)
    return y[:t]


def trunk_layer(x, c, pos, conv_prev, attend, layer_idx, lw):
    b_, t_ = x.shape[:2]
    mod = jnp.einsum('bd,de->be', c, lw['w_ada']) + lw['b_ada']
    sh1, sc1, g1, sh2, sc2, g2 = jnp.split(mod, 6, axis=-1)
    h = x * (1.0 + sc1[:, None]) + sh1[:, None]
    p = jnp.einsum('btd,de->bte', h, lw['w_in'])
    conv_in, gate_b, gate_c = p[..., O_CH:O_CB], p[..., O_CB:O_CC], p[..., O_CC:O_Q]
    q = rope(p[..., O_Q:O_K].reshape(b_, t_, ATT_HEADS, 2, QK_DIM), pos)
    k = rope(p[..., O_K:O_V].reshape(b_, t_, ATT_HEADS, 2, QK_DIM), pos)
    v = p[..., O_V:O_U].reshape(b_, t_, ATT_HEADS, V_DIM)
    u, sv = p[..., O_U:O_SV], p[..., O_SV:IN_W]
    conv_out, conv_state = short_conv(gate_b, gate_c, conv_in, conv_prev, lw['conv_w'])
    lam_init = 0.8 - 0.6 * math.exp(-0.3 * layer_idx)
    lam = (jnp.exp(jnp.sum(lw['lq1'] * lw['lk1']).astype(jnp.float32))
           - jnp.exp(jnp.sum(lw['lq2'] * lw['lk2']).astype(jnp.float32)) + lam_init)
    att = attend(q, k, v, lam)
    att = rms_norm(att, lw['subln_g']) * (1.0 - lam_init)
    chunk_out, chunk_state = chunk_spatial_gate(u, sv, lw['chunk_ln_g'], lw['chunk_ln_b'],
                                                lw['chunk_ws'], lw['chunk_bs'])
    mix = jnp.concatenate([conv_out, att.reshape(b_, t_, ATT_W), chunk_out], -1)
    y = jnp.einsum('bte,ed->btd', mix, lw['w_out'])
    x = layer_norm(DN_ALPHA * x + (1.0 + g1[:, None]) * y, lw['ln1_g'], lw['ln1_b'])
    h2 = x * (1.0 + sc2[:, None]) + sh2[:, None]
    f = moe_ffn(h2.reshape(b_ * t_, D_MODEL), lw['w_router'], lw['b_router'], lw['w_gate'], lw['b_gate'],
                lw['w_up'], lw['b_up'], lw['w_down'], lw['b_down']).reshape(b_, t_, D_MODEL)
    x = layer_norm(DN_ALPHA * x + (1.0 + g2[:, None]) * f, lw['ln2_g'], lw['ln2_b'])
    return x, k, v, conv_state, chunk_state


def setup_inputs(seed: int = 0) -> dict:
    key = jax.random.key(seed)
    ks = jax.random.split(key, 40)
    f32 = jnp.float32

    def nrm(i, shape, scale=1.0):
        return jax.random.normal(ks[i], shape, f32) * scale

    n_pages = PAST_LEN // PAGE_SIZE
    n_used = DEC_BATCH * n_pages
    n_pool = n_used + max(1, n_used // 4)
    page_table = jax.random.permutation(ks[7], n_pool)[:n_used].reshape(DEC_BATCH, n_pages).astype(jnp.int32)
    col = jnp.arange(IN_W)
    col_scale = jnp.where((col >= O_V) & (col < O_U), DN_BETA, 1.0).astype(f32)
    return {
        "x_prompt": nrm(0, (BATCH, SEQ, D_MODEL)),
        "x_sample": nrm(1, (DEC_BATCH, DEC_SEQ, D_MODEL)),
        "c_prompt": nrm(2, (BATCH, D_MODEL)),
        "c_sample": nrm(3, (DEC_BATCH, D_MODEL)),
        "cache_k": nrm(4, (DEPTH, n_pool, PAGE_SIZE, ATT_HEADS, 2, QK_DIM)),
        "cache_v": nrm(5, (DEPTH, n_pool, PAGE_SIZE, ATT_HEADS, V_DIM), DN_BETA),
        "state_conv": nrm(6, (DEPTH, DEC_BATCH, CONV_K - 1, CONV_W)),
        "page_table": page_table,
        "w_ada": nrm(8, (DEPTH, D_MODEL, 6 * D_MODEL), 0.1 * D_MODEL ** -0.5),
        "b_ada": nrm(9, (DEPTH, 6 * D_MODEL), 0.01),
        "w_in": nrm(10, (DEPTH, D_MODEL, IN_W), D_MODEL ** -0.5) * col_scale,
        "conv_w": nrm(11, (DEPTH, CONV_K, CONV_W), CONV_K ** -0.5),
        "lambda_q1": nrm(12, (DEPTH, QK_DIM), 0.1),
        "lambda_k1": nrm(13, (DEPTH, QK_DIM), 0.1),
        "lambda_q2": nrm(14, (DEPTH, QK_DIM), 0.1),
        "lambda_k2": nrm(15, (DEPTH, QK_DIM), 0.1),
        "subln_g": 1.0 + nrm(16, (DEPTH, V_DIM), 0.01),
        "chunk_ln_g": 1.0 + nrm(17, (DEPTH, CHUNK_W), 0.01),
        "chunk_ln_b": nrm(18, (DEPTH, CHUNK_W), 0.01),
        "chunk_ws": nrm(19, (DEPTH, CHUNK_GROUPS, CHUNK, CHUNK), CHUNK ** -0.5),
        "chunk_bs": 1.0 + nrm(20, (DEPTH, CHUNK_GROUPS, CHUNK), 0.01),
        "w_out": nrm(21, (DEPTH, MIX_W, D_MODEL), MIX_W ** -0.5 * DN_BETA),
        "ln1_g": 1.0 + nrm(22, (DEPTH, D_MODEL), 0.01),
        "ln1_b": nrm(23, (DEPTH, D_MODEL), 0.01),
        "w_router": nrm(24, (DEPTH, D_MODEL, N_EXPERTS), D_MODEL ** -0.5),
        "b_router": nrm(25, (DEPTH, N_EXPERTS), 0.01),
        "w_gate": nrm(26, (DEPTH, N_EXPERTS, D_MODEL, D_FF), D_MODEL ** -0.5 * DN_BETA),
        "b_gate": nrm(27, (DEPTH, N_EXPERTS, D_FF), 0.01),
        "w_up": nrm(28, (DEPTH, N_EXPERTS, D_MODEL, D_FF), D_MODEL ** -0.5 * DN_BETA),
        "b_up": nrm(29, (DEPTH, N_EXPERTS, D_FF), 0.01),
        "w_down": nrm(30, (DEPTH, N_EXPERTS, D_FF, D_MODEL), D_FF ** -0.5 * DN_BETA),
        "b_down": nrm(31, (DEPTH, N_EXPERTS, D_MODEL), 0.01),
        "ln2_g": 1.0 + nrm(32, (DEPTH, D_MODEL), 0.01),
        "ln2_b": nrm(33, (DEPTH, D_MODEL), 0.01),
    }


def reference(x_prompt, x_sample, c_prompt, c_sample, cache_k, cache_v, state_conv, page_table,
              w_ada, b_ada, w_in, conv_w, lambda_q1, lambda_k1, lambda_q2, lambda_k2, subln_g,
              chunk_ln_g, chunk_ln_b, chunk_ws, chunk_bs, w_out, ln1_g, ln1_b, w_router, b_router,
              w_gate, b_gate, w_up, b_up, w_down, b_down, ln2_g, ln2_b):
    n_prompt, seq = x_prompt.shape[:2]
    n_dec, dec_seq = x_sample.shape[:2]
    past_len = page_table.shape[1] * PAGE_SIZE
    pos_p = jnp.arange(seq)
    pos_s = past_len + jnp.arange(dec_seq)
    conv0 = jnp.zeros((n_prompt, CONV_K - 1, CONV_W), x_prompt.dtype)
    y_p, y_s = x_prompt, x_sample
    kp_l, vp_l, cp_l, chp_l = [], [], [], []
    ks_l, vs_l, cs_l, chs_l = [], [], [], []
    for l in range(DEPTH):
        lw = {
            'w_ada': w_ada[l], 'b_ada': b_ada[l], 'w_in': w_in[l], 'conv_w': conv_w[l],
            'lq1': lambda_q1[l], 'lk1': lambda_k1[l], 'lq2': lambda_q2[l], 'lk2': lambda_k2[l],
            'subln_g': subln_g[l], 'chunk_ln_g': chunk_ln_g[l], 'chunk_ln_b': chunk_ln_b[l],
            'chunk_ws': chunk_ws[l], 'chunk_bs': chunk_bs[l], 'w_out': w_out[l],
            'ln1_g': ln1_g[l], 'ln1_b': ln1_b[l], 'w_router': w_router[l], 'b_router': b_router[l],
            'w_gate': w_gate[l], 'b_gate': b_gate[l], 'w_up': w_up[l], 'b_up': b_up[l],
            'w_down': w_down[l], 'b_down': b_down[l], 'ln2_g': ln2_g[l], 'ln2_b': ln2_b[l],
        }
        y_p, kp, vp, cp, chp = trunk_layer(y_p, c_prompt, pos_p, conv0, attn_prompt, l, lw)
        kp_l.append(kp); vp_l.append(vp); cp_l.append(cp); chp_l.append(chp)
        k_past = cache_k[l, page_table].reshape(n_dec, past_len, ATT_HEADS, 2, QK_DIM)
        v_past = cache_v[l, page_table].reshape(n_dec, past_len, ATT_HEADS, V_DIM)
        attend_s = functools.partial(attn_sample, k_past=k_past, v_past=v_past)
        y_s, ks_, vs_, cs_, chs_ = trunk_layer(y_s, c_sample, pos_s, state_conv[l], attend_s, l, lw)
        ks_l.append(ks_); vs_l.append(vs_); cs_l.append(cs_); chs_l.append(chs_)
    new_k_prompt = jnp.stack(kp_l)
    new_v_prompt = jnp.stack(vp_l)
    new_conv_prompt = jnp.stack(cp_l)
    new_chunk_v_prompt = jnp.stack(chp_l)
    new_k_sample = jnp.stack(ks_l)
    new_v_sample = jnp.stack(vs_l)
    new_conv_sample = jnp.stack(cs_l)
    new_chunk_v_sample = jnp.stack(chs_l)
    return (y_p, y_s, new_k_prompt, new_v_prompt, new_conv_prompt, new_chunk_v_prompt,
            new_k_sample, new_v_sample, new_conv_sample, new_chunk_v_sample)
```

```python
import functools
import math

import jax
import jax.numpy as jnp
from jax import lax
from jax.experimental import pallas as pl
from jax.experimental.pallas import tpu as pltpu

F32 = jnp.float32
BF16 = jnp.bfloat16

D_MODEL = 1024
CONV_W = 256
CONV_K = 3
ATT_HEADS = 8
QK_DIM = 32
V_DIM = 64
ATT_W = ATT_HEADS * V_DIM
CHUNK_GROUPS = 4
CHUNK_GW = 64
CHUNK_W = CHUNK_GROUPS * CHUNK_GW
CHUNK = 128
O_CH = 0
O_Q = 3 * CONV_W
O_K = O_Q + ATT_HEADS * 2 * QK_DIM
O_V = O_K + ATT_HEADS * 2 * QK_DIM
O_U = O_V + ATT_W
O_SV = O_U + CHUNK_W
IN_W = O_SV + CHUNK_W
N_EXPERTS = 32
TOP_K = 4
D_FF = 1024
SWIGLU_ALPHA = 1.702
SWIGLU_LIMIT = 7.0
PAGE_SIZE = 128
ROPE_THETA = 10000.0
LN_EPS = 1e-5
DEPTH = 4
DN_ALPHA = (2 * DEPTH) ** 0.25

LANES = 128
SUBLANES = 8
VMEM_BYTES_V7X = 64 * 1024 * 1024

TM_PROJ = 512
TQ_ATT = 128
PAGES_PER_STEP = 8
BM_EXPERT = 256
TD_ROWS = 256
NEW_PAD = 128
NEG_BIG = -1e30
VMEM_LIMIT = 56 * 1024 * 1024


def _cparams(sem):
    return pltpu.CompilerParams(dimension_semantics=sem, vmem_limit_bytes=VMEM_LIMIT)


def _split_bf16(a):
    hi = a.astype(BF16)
    lo = (a - hi.astype(F32)).astype(BF16)
    return hi, lo


def _dot(a, b):
    return jnp.dot(a, b, preferred_element_type=F32)


def _dot_split_lhs(a, b_bf16):
    hi, lo = _split_bf16(a)
    return _dot(hi, b_bf16) + _dot(lo, b_bf16)


def _dot3(a, b):
    ah, al = _split_bf16(a)
    bh, bl = _split_bf16(b)
    return _dot(ah, bh) + (_dot(al, bh) + _dot(ah, bl))


def _layer_norm_rows(v, g, b):
    mu = jnp.mean(v, axis=-1, keepdims=True)
    d = v - mu
    var = jnp.mean(d * d, axis=-1, keepdims=True)
    return d * lax.rsqrt(var + LN_EPS) * g + b


def _ada_kernel(c_ref, w_ref, b_ref, o_ref):
    o_ref[0] = _dot3(c_ref[...], w_ref[0]) + b_ref[0]


def _ada_call(c_all, w_ada, b_ada):
    depth, d, n6 = w_ada.shape
    m = c_all.shape[0]
    tn = 1024
    return pl.pallas_call(
        _ada_kernel,
        grid=(depth, n6 // tn),
        in_specs=[
            pl.BlockSpec((m, d), lambda l, j: (0, 0)),
            pl.BlockSpec((1, d, tn), lambda l, j: (l, 0, j)),
            pl.BlockSpec((1, 1, tn), lambda l, j: (l, 0, j)),
        ],
        out_specs=pl.BlockSpec((1, m, tn), lambda l, j: (l, 0, j)),
        out_shape=jax.ShapeDtypeStruct((depth, m, n6), F32),
        compiler_params=_cparams(("arbitrary", "arbitrary")),
        name="ada",
    )(c_all, w_ada, b_ada.reshape(depth, 1, n6))


def _rope_chunks(p, cos_t, sin_t, scale):
    lane = lax.broadcasted_iota(jnp.int32, (1, LANES), 1)
    first_half = (lane % QK_DIM) < (QK_DIM // 2)
    outs = []
    for c in range(p.shape[1] // LANES):
        xc = p[:, c * LANES:(c + 1) * LANES]
        up = pltpu.roll(xc, LANES - QK_DIM // 2, 1)
        dn = pltpu.roll(xc, QK_DIM // 2, 1)
        r = xc * cos_t + jnp.where(first_half, up, dn) * sin_t
        if scale is not None:
            r = r * scale
        outs.append(r)
    return jnp.concatenate(outs, axis=1)


def _group_norm(sv, g, b):
    ri = lax.broadcasted_iota(jnp.int32, (CHUNK_W, CHUNK_W), 0) // CHUNK_GW
    ci = lax.broadcasted_iota(jnp.int32, (CHUNK_W, CHUNK_W), 1) // CHUNK_GW
    avg = jnp.where(ri == ci, 1.0 / CHUNK_GW, 0.0).astype(BF16)
    mu = _dot_split_lhs(sv, avg)
    d = sv - mu
    var = _dot_split_lhs(d * d, avg)
    return d * lax.rsqrt(var + LN_EPS) * g + b


def _chunk_mix(vn, wcat_ref, bsf_ref):
    gid = lax.broadcasted_iota(jnp.int32, (1, CHUNK_W), 1) // CHUNK_GW
    vb = vn.astype(BF16)
    zero = jnp.zeros_like(vb[:CHUNK])
    outs = []
    for c in range(vn.shape[0] // CHUNK):
        vc = vb[c * CHUNK:(c + 1) * CHUNK]
        stack = jnp.concatenate([jnp.where(gid == g, vc, zero) for g in range(CHUNK_GROUPS)], axis=0)
        outs.append(_dot(wcat_ref[...], stack) + bsf_ref[...])
    return jnp.concatenate(outs, axis=0)


def _inproj_kernel(*refs, seq_mode, dec_seq):
    if seq_mode:
        (x_ref, sc_ref, sh_ref, w_ref, cos_ref, sin_ref, cw_ref, cprev_ref, lng_ref, lnb_ref,
         wcat_ref, bsf_ref,
         q_ref, kf_ref, kb_ref, vf_ref, vb_ref, co_ref, cho_ref, cst_ref, chst_ref, zc_ref) = refs
        x = x_ref[0]
        sc = sc_ref[0]
        sh = sh_ref[0]
    else:
        (x_ref, sc_ref, sh_ref, w_ref, cos_ref, sin_ref, cw_ref, p1_ref, p2_ref, lng_ref, lnb_ref,
         wcat_ref, bsf_ref,
         q_ref, kf_ref, kb_ref, vf_ref, vb_ref, co_ref, cho_ref, cst_ref, chst_ref) = refs
        x = x_ref[...]
        sc = sc_ref[...]
        sh = sh_ref[...]
    tm = x.shape[0]
    h = (x * (1.0 + sc) + sh).astype(BF16)

    pc = _dot(h, w_ref[:, O_CH:O_Q])
    conv_in, gate_b, gate_c = pc[:, :CONV_W], pc[:, CONV_W:2 * CONV_W], pc[:, 2 * CONV_W:]
    z = gate_c * conv_in
    row = lax.broadcasted_iota(jnp.int32, (tm, 1), 0)
    z1 = pltpu.roll(z, 1, 0)
    z2 = pltpu.roll(z, 2, 0)
    if seq_mode:
        @pl.when(pl.program_id(1) == 0)
        def _():
            zc_ref[0:2, :] = cprev_ref[0]
        c0 = zc_ref[0:1, :]
        c1 = zc_ref[1:2, :]
        z1 = jnp.where(row == 0, c1, z1)
        z2 = jnp.where(row == 0, c0, jnp.where(row == 1, c1, z2))
        zc_ref[0:2, :] = z[tm - 2:tm, :]
        cst_ref[0] = z[tm - 2:tm, :]
    else:
        pos_in_seq = row % dec_seq
        z1 = jnp.where(pos_in_seq >= 1, z1, p1_ref[...])
        z2 = jnp.where(pos_in_seq >= 2, z2, p2_ref[...])
        cst_ref[...] = z
    y = cw_ref[0:1, :] * z2 + cw_ref[1:2, :] * z1 + cw_ref[2:3, :] * z
    conv_out = (gate_b * y).astype(BF16)

    pqk = _dot(h, w_ref[:, O_Q:O_V])
    cos_t = cos_ref[...]
    sin_t = sin_ref[...]
    nq = O_K - O_Q
    qr = _rope_chunks(pqk[:, :nq], cos_t, sin_t, QK_DIM ** -0.5)
    kr = _rope_chunks(pqk[:, nq:], cos_t, sin_t, None)
    pv = _dot(h, w_ref[:, O_V:O_U])

    pu = _dot(h, w_ref[:, O_U:IN_W])
    u, sv = pu[:, :CHUNK_W], pu[:, CHUNK_W:]
    vn = _group_norm(sv, lng_ref[...], lnb_ref[...])
    chunk_out = (u * _chunk_mix(vn, wcat_ref, bsf_ref)).astype(BF16)

    if seq_mode:
        q_ref[0] = qr.astype(BF16)
        kf_ref[0] = kr
        kb_ref[0] = kr.astype(BF16)
        vf_ref[0] = pv
        vb_ref[0] = pv.astype(BF16)
        co_ref[0] = conv_out
        cho_ref[0] = chunk_out
        chst_ref[0] = vn[tm - CHUNK:tm, :]
    else:
        q_ref[...] = qr.astype(BF16)
        kf_ref[...] = kr
        kb_ref[...] = kr.astype(BF16)
        vf_ref[...] = pv
        vb_ref[...] = pv.astype(BF16)
        co_ref[...] = conv_out
        cho_ref[...] = chunk_out
        chst_ref[...] = vn


def _inproj_prompt(x, sc1, sh1, w_in_b, cos_t, sin_t, conv_w, conv_prev, ln_g, ln_b, wcat, bsf):
    b_, t_, d = x.shape
    tm = min(TM_PROJ, t_)
    nt = t_ // tm
    const2 = lambda b, i: (0, 0)
    tok3 = lambda b, i: (b, i, 0)
    bat3 = lambda b, i: (b, 0, 0)
    qk_w = ATT_HEADS * 2 * QK_DIM
    outs = pl.pallas_call(
        functools.partial(_inproj_kernel, seq_mode=True, dec_seq=0),
        grid=(b_, nt),
        in_specs=[
            pl.BlockSpec((1, tm, d), tok3),
            pl.BlockSpec((1, 1, d), bat3),
            pl.BlockSpec((1, 1, d), bat3),
            pl.BlockSpec((d, IN_W), const2),
            pl.BlockSpec((tm, LANES), lambda b, i: (i, 0)),
            pl.BlockSpec((tm, LANES), lambda b, i: (i, 0)),
            pl.BlockSpec((SUBLANES, CONV_W), const2),
            pl.BlockSpec((1, CONV_K - 1, CONV_W), bat3),
            pl.BlockSpec((1, CHUNK_W), const2),
            pl.BlockSpec((1, CHUNK_W), const2),
            pl.BlockSpec((CHUNK, CHUNK_GROUPS * CHUNK), const2),
            pl.BlockSpec((CHUNK, CHUNK_W), const2),
        ],
        out_specs=[
            pl.BlockSpec((1, tm, qk_w), tok3),
            pl.BlockSpec((1, tm, qk_w), tok3),
            pl.BlockSpec((1, tm, qk_w), tok3),
            pl.BlockSpec((1, tm, ATT_W), tok3),
            pl.BlockSpec((1, tm, ATT_W), tok3),
            pl.BlockSpec((1, tm, CONV_W), tok3),
            pl.BlockSpec((1, tm, CHUNK_W), tok3),
            pl.BlockSpec((1, CONV_K - 1, CONV_W), bat3),
            pl.BlockSpec((1, CHUNK, CHUNK_W), bat3),
        ],
        out_shape=[
            jax.ShapeDtypeStruct((b_, t_, qk_w), BF16),
            jax.ShapeDtypeStruct((b_, t_, qk_w), F32),
            jax.ShapeDtypeStruct((b_, t_, qk_w), BF16),
            jax.ShapeDtypeStruct((b_, t_, ATT_W), F32),
            jax.ShapeDtypeStruct((b_, t_, ATT_W), BF16),
            jax.ShapeDtypeStruct((b_, t_, CONV_W), BF16),
            jax.ShapeDtypeStruct((b_, t_, CHUNK_W), BF16),
            jax.ShapeDtypeStruct((b_, CONV_K - 1, CONV_W), F32),
            jax.ShapeDtypeStruct((b_, CHUNK, CHUNK_W), F32),
        ],
        scratch_shapes=[pltpu.VMEM((SUBLANES, CONV_W), F32)],
        compiler_params=_cparams(("arbitrary", "arbitrary")),
        name="inproj_prompt",
    )(x, sc1, sh1, w_in_b, cos_t, sin_t, conv_w, conv_prev, ln_g, ln_b, wcat, bsf)
    return outs


def _inproj_sample(x2d, sc1, sh1, w_in_b, cos_t, sin_t, conv_w, prev1, prev2, ln_g, ln_b, wcat, bsf,
                   dec_seq):
    n, d = x2d.shape
    qk_w = ATT_HEADS * 2 * QK_DIM
    full = lambda shape: pl.BlockSpec(shape, lambda i: tuple(0 for _ in shape))
    outs = pl.pallas_call(
        functools.partial(_inproj_kernel, seq_mode=False, dec_seq=dec_seq),
        grid=(1,),
        in_specs=[
            full((n, d)), full((n, d)), full((n, d)), full((d, IN_W)),
            full((n, LANES)), full((n, LANES)), full((SUBLANES, CONV_W)),
            full((n, CONV_W)), full((n, CONV_W)), full((1, CHUNK_W)), full((1, CHUNK_W)),
            full((CHUNK, CHUNK_GROUPS * CHUNK)), full((CHUNK, CHUNK_W)),
        ],
        out_specs=[
            full((n, qk_w)), full((n, qk_w)), full((n, qk_w)), full((n, ATT_W)), full((n, ATT_W)),
            full((n, CONV_W)), full((n, CHUNK_W)), full((n, CONV_W)), full((n, CHUNK_W)),
        ],
        out_shape=[
            jax.ShapeDtypeStruct((n, qk_w), BF16),
            jax.ShapeDtypeStruct((n, qk_w), F32),
            jax.ShapeDtypeStruct((n, qk_w), BF16),
            jax.ShapeDtypeStruct((n, ATT_W), F32),
            jax.ShapeDtypeStruct((n, ATT_W), BF16),
            jax.ShapeDtypeStruct((n, CONV_W), BF16),
            jax.ShapeDtypeStruct((n, CHUNK_W), BF16),
            jax.ShapeDtypeStruct((n, CONV_W), F32),
            jax.ShapeDtypeStruct((n, CHUNK_W), F32),
        ],
        compiler_params=_cparams(("arbitrary",)),
        name="inproj_sample",
    )(x2d, sc1, sh1, w_in_b, cos_t, sin_t, conv_w, prev1, prev2, ln_g, ln_b, wcat, bsf)
    return outs


def _lambda_from(lam_ref):
    lp = lam_ref[...]
    s1 = jnp.sum(lp[0:1, :] * lp[1:2, :], axis=1, keepdims=True)
    s2 = jnp.sum(lp[2:3, :] * lp[3:4, :], axis=1, keepdims=True)
    lam_init = lp[4:5, 0:1]
    return jnp.exp(s1) - jnp.exp(s2) + lam_init, lam_init


def _attn_prompt_kernel(q_ref, k_ref, v_ref, lam_ref, g_ref, o_ref, qs_ref, m_ref, a0_ref, a1_ref):
    tq = q_ref.shape[1]
    qi = pl.program_id(2)
    lane = lax.broadcasted_iota(jnp.int32, (1, LANES), 1)
    low = lane < V_DIM

    q = q_ref[0]
    zero = jnp.zeros_like(q)
    for hm in range(4):
        qs_ref[hm * tq:(hm + 1) * tq, :] = jnp.where(lane // QK_DIM == hm, q, zero)
    m_ref[...] = jnp.full(m_ref.shape, NEG_BIG, F32)
    a0_ref[...] = jnp.zeros(a0_ref.shape, F32)
    a1_ref[...] = jnp.zeros(a1_ref.shape, F32)

    def step(ki, masked):
        start = pl.multiple_of(ki * tq, tq)
        kt = k_ref[0, pl.ds(start, tq), :]
        vt = v_ref[0, pl.ds(start, tq), :]
        s = lax.dot_general(qs_ref[...], kt, (((1,), (1,)), ((), ())), preferred_element_type=F32)
        if masked:
            r = lax.broadcasted_iota(jnp.int32, (4 * tq, tq), 0) % tq
            c = lax.broadcasted_iota(jnp.int32, (4 * tq, tq), 1)
            s = jnp.where(c <= r, s, NEG_BIG)
        m_old = m_ref[...]
        m_new = jnp.maximum(m_old, jnp.max(s, axis=1, keepdims=True))
        alpha = jnp.exp(m_old - m_new)
        p = jnp.exp(s - m_new).astype(BF16)
        m_ref[...] = m_new
        one = jnp.ones_like(vt)
        pv0 = _dot(p[:2 * tq], jnp.where(low, vt, one))
        pv1 = _dot(p[2 * tq:], jnp.where(low, one, vt))
        a0_ref[...] = alpha[:2 * tq] * a0_ref[...] + pv0
        a1_ref[...] = alpha[2 * tq:] * a1_ref[...] + pv1

    def body(ki, carry):
        step(ki, False)
        return carry

    lax.fori_loop(0, qi, body, 0)
    step(qi, True)

    lam, lam_init = _lambda_from(lam_ref)
    a0 = a0_ref[...]
    a1 = a1_ref[...]
    r0 = a0 / pltpu.roll(a0, V_DIM, 1)
    r1 = a1 / pltpu.roll(a1, V_DIM, 1)
    d0 = r0[:tq] - lam * r0[tq:]
    d1 = r1[:tq] - lam * r1[tq:]
    att = jnp.where(low, d0, d1)
    sq = att * att
    s_low = jnp.sum(jnp.where(low, sq, 0.0), axis=1, keepdims=True)
    s_all = jnp.sum(sq, axis=1, keepdims=True)
    ms = jnp.where(low, s_low, s_all - s_low) * (1.0 / V_DIM)
    o_ref[0] = (att * lax.rsqrt(ms + LN_EPS) * g_ref[...] * (1.0 - lam_init)).astype(o_ref.dtype)


def _attn_prompt(q, k, v, lamp, g2):
    b_, t_, _ = q.shape
    tq = min(TQ_ATT, t_)
    nq = t_ // tq
    npair = ATT_HEADS // 2
    return pl.pallas_call(
        _attn_prompt_kernel,
        grid=(b_, npair, nq),
        in_specs=[
            pl.BlockSpec((1, tq, LANES), lambda b, h, i: (b, i, h)),
            pl.BlockSpec((1, t_, LANES), lambda b, h, i: (b, 0, h)),
            pl.BlockSpec((1, t_, LANES), lambda b, h, i: (b, 0, h)),
            pl.BlockSpec((SUBLANES, LANES), lambda b, h, i: (0, 0)),
            pl.BlockSpec((1, LANES), lambda b, h, i: (0, 0)),
        ],
        out_specs=pl.BlockSpec((1, tq, LANES), lambda b, h, i: (b, i, h)),
        out_shape=jax.ShapeDtypeStruct((b_, t_, ATT_W), BF16),
        scratch_shapes=[
            pltpu.VMEM((4 * tq, LANES), BF16),
            pltpu.VMEM((4 * tq, 1), F32),
            pltpu.VMEM((2 * tq, LANES), F32),
            pltpu.VMEM((2 * tq, LANES), F32),
        ],
        compiler_params=_cparams(("arbitrary", "arbitrary", "arbitrary")),
        name="attn_prompt",
    )(q, k, v, lamp, g2)


def _attn_sample_kernel(l_ref, pt_ref, *refs, pps, dec_seq):
    k_refs = refs[:pps]
    v_refs = refs[pps:2 * pps]
    qbd_ref, kn_ref, vn_ref, lam_ref, g_ref, o_ref, m_ref, l_acc_ref, acc_ref = refs[2 * pps:]
    j = pl.program_id(1)
    nrow = qbd_ref.shape[1]
    half = nrow // 2

    @pl.when(j == 0)
    def _():
        m_ref[...] = jnp.full(m_ref.shape, NEG_BIG, F32)
        l_acc_ref[...] = jnp.zeros(l_acc_ref.shape, F32)
        acc_ref[...] = jnp.zeros(acc_ref.shape, F32)

    qbd = qbd_ref[0]

    def online(s, vals):
        m_old = m_ref[...]
        m_new = jnp.maximum(m_old, jnp.max(s, axis=1, keepdims=True))
        alpha = jnp.exp(m_old - m_new)
        p = jnp.exp(s - m_new)
        m_ref[...] = m_new
        l_acc_ref[...] = alpha * l_acc_ref[...] + jnp.sum(p, axis=1, keepdims=True)
        pb = p.astype(BF16)
        w = vals[0].shape[0]
        pv = _dot(pb[:, 0:w], vals[0])
        for r in range(1, len(vals)):
            pv = pv + _dot(pb[:, r * w:(r + 1) * w], vals[r])
        acc_ref[...] = alpha * acc_ref[...] + pv

    nt = (((1,), (1,)), ((), ()))
    s_parts = [lax.dot_general(qbd, k_refs[r][0, 0].astype(BF16), nt, preferred_element_type=F32)
               for r in range(pps)]
    online(jnp.concatenate(s_parts, axis=1), [v_refs[r][0, 0].astype(BF16) for r in range(pps)])

    @pl.when(j == pl.num_programs(1) - 1)
    def _():
        kn = kn_ref[0].astype(BF16)
        s = lax.dot_general(qbd, kn, nt, preferred_element_type=F32)
        r_ = lax.broadcasted_iota(jnp.int32, s.shape, 0)
        c_ = lax.broadcasted_iota(jnp.int32, s.shape, 1)
        qpos = (r_ % half) // ATT_HEADS
        s = jnp.where(c_ <= qpos, s, NEG_BIG)
        online(s, [vn_ref[0].astype(BF16)])

        lam, lam_init = _lambda_from(lam_ref)
        o = acc_ref[...] / l_acc_ref[...]
        d = o[:half] - lam * o[half:]
        rr = lax.broadcasted_iota(jnp.int32, d.shape, 0) % ATT_HEADS
        cc = lax.broadcasted_iota(jnp.int32, d.shape, 1) // V_DIM
        d = jnp.where(rr == cc, d, 0.0)
        ms = jnp.sum(d * d, axis=1, keepdims=True) * (1.0 / V_DIM)
        dn = d * lax.rsqrt(ms + LN_EPS) * g_ref[...] * (1.0 - lam_init)
        o_ref[0] = jnp.sum(dn.reshape(dec_seq, ATT_HEADS, ATT_W), axis=1)


def _attn_sample(layer, page_table, cache_k4, cache_v4, qbd, kn_pad, vn_pad, lamp, g8, dec_seq):
    n_dec, n_pages = page_table.shape
    pps = PAGES_PER_STEP
    while n_pages % pps:
        pps //= 2
    nrow = qbd.shape[1]
    qkw = cache_k4.shape[-1]

    def page_map(r):
        return lambda b, j, l_ref, pt_ref: (l_ref[0], pt_ref[b * n_pages + j * pps + r], 0, 0)

    seq3 = lambda b, j, l_ref, pt_ref: (b, 0, 0)
    const2 = lambda b, j, l_ref, pt_ref: (0, 0)
    in_specs = ([pl.BlockSpec((1, 1, PAGE_SIZE, qkw), page_map(r)) for r in range(pps)]
                + [pl.BlockSpec((1, 1, PAGE_SIZE, ATT_W), page_map(r)) for r in range(pps)]
                + [pl.BlockSpec((1, nrow, qkw), seq3),
                   pl.BlockSpec((1, NEW_PAD, qkw), seq3),
                   pl.BlockSpec((1, NEW_PAD, ATT_W), seq3),
                   pl.BlockSpec((SUBLANES, LANES), const2),
                   pl.BlockSpec((1, ATT_W), const2)])
    grid_spec = pltpu.PrefetchScalarGridSpec(
        num_scalar_prefetch=2,
        grid=(n_dec, n_pages // pps),
        in_specs=in_specs,
        out_specs=pl.BlockSpec((1, dec_seq, ATT_W), seq3),
        scratch_shapes=[
            pltpu.VMEM((nrow, 1), F32),
            pltpu.VMEM((nrow, 1), F32),
            pltpu.VMEM((nrow, ATT_W), F32),
        ],
    )
    return pl.pallas_call(
        functools.partial(_attn_sample_kernel, pps=pps, dec_seq=dec_seq),
        grid_spec=grid_spec,
        out_shape=jax.ShapeDtypeStruct((n_dec, dec_seq, ATT_W), F32),
        compiler_params=_cparams(("arbitrary", "arbitrary")),
        name="attn_sample",
    )(layer, page_table.reshape(-1), *([cache_k4] * pps), *([cache_v4] * pps),
      qbd, kn_pad, vn_pad, lamp, g8)


def _outproj_kernel(x_ref, co_ref, att_ref, cho_ref, g1_ref, sc2_ref, sh2_ref, wo_ref, lg_ref, lb_ref,
                    wr_ref, br_ref, cin_ref,
                    x1_ref, h2_ref, route_ref, gate_ref, cnt_ref, carry_ref, *, batched):
    first = (pl.program_id(0) == 0) & (pl.program_id(1) == 0)

    @pl.when(first)
    def _():
        carry_ref[...] = cin_ref[...]

    if batched:
        x, co, att, cho = x_ref[0], co_ref[0], att_ref[0], cho_ref[0]
        g1, sc2, sh2 = g1_ref[0], sc2_ref[0], sh2_ref[0]
    else:
        x, co, att, cho = x_ref[...], co_ref[...], att_ref[...], cho_ref[...]
        g1, sc2, sh2 = g1_ref[...], sc2_ref[...], sh2_ref[...]
    tm = x.shape[0]
    mix = jnp.concatenate([co, att, cho], axis=1)
    y = _dot(mix, wo_ref[...])
    x1 = _layer_norm_rows(DN_ALPHA * x + (1.0 + g1) * y, lg_ref[...], lb_ref[...])
    h2 = x1 * (1.0 + sc2) + sh2

    logits = _dot3(h2, wr_ref[...]) + br_ref[...]
    lane = lax.broadcasted_iota(jnp.int32, (1, LANES), 1)
    lane_f = lane.astype(F32)
    work = logits
    vals, hots, idxs = [], [], []
    for _ in range(TOP_K):
        mx = jnp.max(work, axis=1, keepdims=True)
        idx = jnp.min(jnp.where(work == mx, lane_f, float(LANES)), axis=1, keepdims=True)
        hot = lane_f == idx
        work = jnp.where(hot, -3e38, work)
        vals.append(mx)
        hots.append(hot)
        idxs.append(idx)
    exps = [jnp.exp(v - vals[0]) for v in vals]
    inv = 1.0 / (exps[0] + exps[1] + exps[2] + exps[3])

    any_hot = hots[0] | hots[1] | hots[2] | hots[3]
    a_b = jnp.where(any_hot, 1.0, 0.0).astype(BF16)
    ri = lax.broadcasted_iota(jnp.int32, (tm, tm), 0)
    ci = lax.broadcasted_iota(jnp.int32, (tm, tm), 1)
    stril = jnp.where(ci < ri, 1.0, 0.0).astype(BF16)
    before = _dot(stril, a_b) + carry_ref[0:1, :]
    new_carry = carry_ref[0:1, :] + jnp.sum(a_b.astype(F32), axis=0, keepdims=True)
    carry_ref[0:1, :] = new_carry

    route = jnp.zeros((tm, LANES), F32)
    gates = jnp.zeros((tm, LANES), F32)
    for k in range(TOP_K):
        rank_k = jnp.sum(jnp.where(hots[k], before, 0.0), axis=1, keepdims=True)
        route = jnp.where(lane == k, idxs[k], route)
        route = jnp.where(lane == TOP_K + k, rank_k, route)
        gates = jnp.where(lane == k, exps[k] * inv, gates)

    if batched:
        x1_ref[0] = x1
        h2_ref[0] = h2
        route_ref[0] = route.astype(jnp.int32)
        gate_ref[0] = gates
    else:
        x1_ref[...] = x1
        h2_ref[...] = h2
        route_ref[...] = route.astype(jnp.int32)
        gate_ref[...] = gates
    cnt_ref[...] = jnp.broadcast_to(new_carry, cnt_ref.shape)


def _outproj(x, co, att, cho, g1, sc2, sh2, w_out_b, ln_g, ln_b, wr_pad, br_pad, carry_in, batched):
    if batched:
        b_, t_, d = x.shape
        tm = min(TM_PROJ, t_)
        grid = (b_, t_ // tm)
        tok = lambda w: pl.BlockSpec((1, tm, w), lambda b, i: (b, i, 0))
        mod = pl.BlockSpec((1, 1, d), lambda b, i: (b, 0, 0))
        oshape = lambda w, dt: jax.ShapeDtypeStruct((b_, t_, w), dt)
    else:
        n, d = x.shape
        tm = n
        grid = (1, 1)
        tok = lambda w: pl.BlockSpec((n, w), lambda b, i: (0, 0))
        mod = tok(d)
        oshape = lambda w, dt: jax.ShapeDtypeStruct((n, w), dt)
    const2 = lambda b, i: (0, 0)
    return pl.pallas_call(
        functools.partial(_outproj_kernel, batched=batched),
        grid=grid,
        in_specs=[
            tok(d), tok(CONV_W), tok(ATT_W), tok(CHUNK_W), mod, mod, mod,
            pl.BlockSpec((d, d), const2),
            pl.BlockSpec((1, d), const2),
            pl.BlockSpec((1, d), const2),
            pl.BlockSpec((d, LANES), const2),
            pl.BlockSpec((1, LANES), const2),
            pl.BlockSpec((SUBLANES, LANES), const2),
        ],
        out_specs=[tok(d), tok(d), tok(LANES), tok(LANES), pl.BlockSpec((SUBLANES, LANES), const2)],
        out_shape=[oshape(d, F32), oshape(d, F32), oshape(LANES, jnp.int32), oshape(LANES, F32),
                   jax.ShapeDtypeStruct((SUBLANES, LANES), F32)],
        scratch_shapes=[pltpu.VMEM((SUBLANES, LANES), F32)],
        compiler_params=_cparams(("arbitrary", "arbitrary")),
        name="outproj_prompt" if batched else "outproj_sample",
    )(x, co, att, cho, g1, sc2, sh2, w_out_b, ln_g, ln_b, wr_pad, br_pad, carry_in)


def _dispatch_kernel(pstart_ref, route_ref, h_ref, xs_in_ref, xs_ref, sem):
    del xs_in_ref
    td = h_ref.shape[0]

    def row_copy(r, dst):
        return pltpu.make_async_copy(h_ref.at[pl.ds(r, 1)], xs_ref.at[pl.ds(dst, 1)], sem)

    def issue(r, carry):
        for k in range(TOP_K):
            e = route_ref[0, 0, r * 2 * TOP_K + k]
            rk = route_ref[0, 0, r * 2 * TOP_K + TOP_K + k]
            row_copy(r, pstart_ref[e] + rk).start()
        return carry

    lax.fori_loop(0, td, issue, 0)

    def drain(i, carry):
        row_copy(0, 0).wait()
        return carry

    lax.fori_loop(0, td * TOP_K, drain, 0)


def _dispatch(pstart, route8, h2d, xs):
    n, d = h2d.shape
    td = min(TD_ROWS, n)
    nt = n // td
    route_t = route8.reshape(nt, 1, td * 2 * TOP_K)
    grid_spec = pltpu.PrefetchScalarGridSpec(
        num_scalar_prefetch=1,
        grid=(nt,),
        in_specs=[
            pl.BlockSpec((1, 1, td * 2 * TOP_K), lambda i, ps: (i, 0, 0), memory_space=pltpu.SMEM),
            pl.BlockSpec((td, d), lambda i, ps: (i, 0)),
            pl.BlockSpec(memory_space=pl.ANY),
        ],
        out_specs=pl.BlockSpec(memory_space=pl.ANY),
        scratch_shapes=[pltpu.SemaphoreType.DMA(())],
    )
    return pl.pallas_call(
        _dispatch_kernel,
        grid_spec=grid_spec,
        out_shape=jax.ShapeDtypeStruct(xs.shape, xs.dtype),
        input_output_aliases={3: 0},
        compiler_params=_cparams(("arbitrary",)),
        name="moe_dispatch",
    )(pstart, route_t, h2d, xs)


def _expert_kernel(l_ref, be_ref, nu_ref, x_ref, wg_ref, bg_ref, wu_ref, bu_ref, wd_ref, bd_ref,
                   y_ref, wgb_ref, wub_ref, wdb_ref):
    b = pl.program_id(0)
    changed = (b == 0) | (be_ref[b] != be_ref[jnp.maximum(b - 1, 0)])

    @pl.when(changed)
    def _():
        wgb_ref[...] = wg_ref[0, 0].astype(BF16)
        wub_ref[...] = wu_ref[0, 0].astype(BF16)
        wdb_ref[...] = wd_ref[0, 0].astype(BF16)

    @pl.when(b < nu_ref[0])
    def _():
        x = x_ref[...].astype(BF16)
        g = jnp.minimum(_dot(x, wgb_ref[...]) + bg_ref[0, 0], SWIGLU_LIMIT)
        u = jnp.clip(_dot(x, wub_ref[...]) + bu_ref[0, 0], -SWIGLU_LIMIT, SWIGLU_LIMIT)
        hmid = (u + 1.0) * (g * jax.nn.sigmoid(g * SWIGLU_ALPHA))
        y_ref[...] = _dot(hmid.astype(BF16), wdb_ref[...]) + bd_ref[0, 0]

    @pl.when(b >= nu_ref[0])
    def _():
        y_ref[...] = jnp.zeros(y_ref.shape, y_ref.dtype)


def _experts(layer, block_e, n_used, xs, w_gate, b_gate, w_up, b_up, w_down, b_down):
    n_rows, d = xs.shape
    bm = BM_EXPERT
    nb = n_rows // bm
    dff = w_gate.shape[-1]
    wmap = lambda b, l, be, nu: (l[0], be[b], 0, 0)
    grid_spec = pltpu.PrefetchScalarGridSpec(
        num_scalar_prefetch=3,
        grid=(nb,),
        in_specs=[
            pl.BlockSpec((bm, d), lambda b, l, be, nu: (b, 0)),
            pl.BlockSpec((1, 1, d, dff), wmap),
            pl.BlockSpec((1, 1, 1, dff), wmap),
            pl.BlockSpec((1, 1, d, dff), wmap),
            pl.BlockSpec((1, 1, 1, dff), wmap),
            pl.BlockSpec((1, 1, dff, d), wmap),
            pl.BlockSpec((1, 1, 1, d), wmap),
        ],
        out_specs=pl.BlockSpec((bm, d), lambda b, l, be, nu: (b, 0)),
        scratch_shapes=[pltpu.VMEM((d, dff), BF16), pltpu.VMEM((d, dff), BF16), pltpu.VMEM((dff, d), BF16)],
    )
    depth, ne = b_gate.shape[:2]
    return pl.pallas_call(
        _expert_kernel,
        grid_spec=grid_spec,
        out_shape=jax.ShapeDtypeStruct((n_rows, d), F32),
        compiler_params=_cparams(("arbitrary",)),
        name="moe_experts",
    )(layer, block_e, n_used, xs, w_gate, b_gate.reshape(depth, ne, 1, dff), w_up,
      b_up.reshape(depth, ne, 1, dff), w_down, b_down.reshape(depth, ne, 1, d))


def _combine_kernel(pstart_ref, route_ref, gate_ref, x1_ref, g2_ref, lg_ref, lb_ref, yb_ref,
                    o_ref, buf_ref, sem):
    tc = x1_ref.shape[0]

    def row_copy(src, k, r):
        return pltpu.make_async_copy(yb_ref.at[pl.ds(src, 1)], buf_ref.at[k, pl.ds(r, 1)], sem)

    def issue(r, carry):
        for k in range(TOP_K):
            e = route_ref[0, 0, r * 2 * TOP_K + k]
            rk = route_ref[0, 0, r * 2 * TOP_K + TOP_K + k]
            row_copy(pstart_ref[e] + rk, k, r).start()
        return carry

    lax.fori_loop(0, tc, issue, 0)

    def drain(i, carry):
        row_copy(0, 0, 0).wait()
        return carry

    lax.fori_loop(0, tc * TOP_K, drain, 0)

    gates = gate_ref[...]
    f = gates[:, 0:1] * buf_ref[0]
    for k in range(1, TOP_K):
        f = f + gates[:, k:k + 1] * buf_ref[k]
    o_ref[...] = _layer_norm_rows(DN_ALPHA * x1_ref[...] + (1.0 + g2_ref[...]) * f,
                                  lg_ref[...], lb_ref[...])


def _combine(pstart, route8, gates, x1_2d, g2, ln_g, ln_b, yb, rows_per_mod):
    n, d = x1_2d.shape
    tc = min(TD_ROWS, n)
    nt = n // tc
    route_t = route8.reshape(nt, 1, tc * 2 * TOP_K)
    if rows_per_mod == 1:
        g2_spec = pl.BlockSpec((tc, d), lambda i, ps: (i, 0))
        g2_in = g2
    else:
        tiles_per_mod = rows_per_mod // tc
        g2_spec = pl.BlockSpec((1, 1, d), lambda i, ps: (i // tiles_per_mod, 0, 0))
        g2_in = g2.reshape(g2.shape[0], 1, d)
    grid_spec = pltpu.PrefetchScalarGridSpec(
        num_scalar_prefetch=1,
        grid=(nt,),
        in_specs=[
            pl.BlockSpec((1, 1, tc * 2 * TOP_K), lambda i, ps: (i, 0, 0), memory_space=pltpu.SMEM),
            pl.BlockSpec((tc, LANES), lambda i, ps: (i, 0)),
            pl.BlockSpec((tc, d), lambda i, ps: (i, 0)),
            g2_spec,
            pl.BlockSpec((1, d), lambda i, ps: (0, 0)),
            pl.BlockSpec((1, d), lambda i, ps: (0, 0)),
            pl.BlockSpec(memory_space=pl.ANY),
        ],
        out_specs=pl.BlockSpec((tc, d), lambda i, ps: (i, 0)),
        scratch_shapes=[pltpu.VMEM((TOP_K, tc, d), F32), pltpu.SemaphoreType.DMA(())],
    )
    kern = _combine_kernel if rows_per_mod == 1 else _combine_kernel_bcast
    return pl.pallas_call(
        kern,
        grid_spec=grid_spec,
        out_shape=jax.ShapeDtypeStruct((n, d), F32),
        compiler_params=_cparams(("arbitrary",)),
        name="moe_combine",
    )(pstart, route_t, gates, x1_2d, g2_in, ln_g, ln_b, yb)


def _combine_kernel_bcast(pstart_ref, route_ref, gate_ref, x1_ref, g2_ref, lg_ref, lb_ref, yb_ref,
                          o_ref, buf_ref, sem):
    _combine_kernel(pstart_ref, route_ref, gate_ref, x1_ref, g2_ref.at[0], lg_ref, lb_ref, yb_ref,
                    o_ref, buf_ref, sem)


def _rope_tables(pos):
    half = QK_DIM // 2
    inv = ROPE_THETA ** (-jnp.arange(half, dtype=F32) / half)
    ang = pos.astype(F32)[:, None] * inv[None, :]
    cos = jnp.cos(ang)
    sin = jnp.sin(ang)
    reps = LANES // QK_DIM
    cos_t = jnp.tile(jnp.concatenate([cos, cos], axis=1), (1, reps))
    sin_t = jnp.tile(jnp.concatenate([-sin, sin], axis=1), (1, reps))
    return cos_t, sin_t


def _pad_rows(a, rows):
    return jnp.concatenate([a, jnp.zeros((rows - a.shape[0],) + a.shape[1:], a.dtype)], axis=0)


def kernel(x_prompt, x_sample, c_prompt, c_sample, cache_k, cache_v, state_conv, page_table, w_ada, b_ada, w_in, conv_w, lambda_q1, lambda_k1, lambda_q2, lambda_k2, subln_g, chunk_ln_g, chunk_ln_b, chunk_ws, chunk_bs, w_out, ln1_g, ln1_b, w_router, b_router, w_gate, b_gate, w_up, b_up, w_down, b_down, ln2_g, ln2_b):
    n_prompt, seq, d = x_prompt.shape
    n_dec, dec_seq, _ = x_sample.shape
    depth = w_in.shape[0]
    n_pool = cache_k.shape[1]
    n_pages = page_table.shape[1]
    past_len = n_pages * PAGE_SIZE
    ns = n_dec * dec_seq
    np_tok = n_prompt * seq
    assert dec_seq >= CONV_K - 1 and dec_seq <= SUBLANES and CHUNK % dec_seq == 0
    assert ns % CHUNK == 0 and seq % CHUNK == 0

    c_all = jnp.concatenate([c_prompt, c_sample], axis=0)
    m_rows = -(-c_all.shape[0] // SUBLANES) * SUBLANES
    mod_all = _ada_call(_pad_rows(c_all, m_rows), w_ada, b_ada)

    cos_p, sin_p = _rope_tables(jnp.arange(seq))
    cos_s, sin_s = _rope_tables(jnp.tile(past_len + jnp.arange(dec_seq), n_dec))

    cache_k4 = cache_k.reshape(depth, n_pool, PAGE_SIZE, ATT_HEADS * 2 * QK_DIM)
    cache_v4 = cache_v.reshape(depth, n_pool, PAGE_SIZE, ATT_W)
    conv0 = jnp.zeros((n_prompt, CONV_K - 1, CONV_W), F32)
    lane = jnp.arange(LANES)
    tok_pos = jnp.arange(ns) % dec_seq

    nrow = 2 * dec_seq * ATT_HEADS
    r_map = jnp.arange(nrow) // (dec_seq * ATT_HEADS)
    r_q = (jnp.arange(nrow) // ATT_HEADS) % dec_seq
    r_h = jnp.arange(nrow) % ATT_HEADS
    col = jnp.arange(ATT_HEADS * 2 * QK_DIM)
    qbd_mask = ((col[None, :] // (2 * QK_DIM)) == r_h[:, None]) & (((col[None, :] // QK_DIM) % 2) == r_map[:, None])

    n_assign = (np_tok + ns) * TOP_K
    n_blocks = -(-n_assign // BM_EXPERT) + N_EXPERTS
    n_rows = n_blocks * BM_EXPERT

    y_p, y_s = x_prompt, x_sample.reshape(ns, d)
    outs = {k: [] for k in ("kp", "vp", "cp", "chp", "ks", "vs", "cs", "chs")}
    for l in range(depth):
        layer = jnp.full((1,), l, jnp.int32)
        lam_init = 0.8 - 0.6 * math.exp(-0.3 * l)
        mod = mod_all[l]
        mp = mod[:n_prompt].reshape(n_prompt, 6, 1, d)
        ms = jnp.repeat(mod[n_prompt:n_prompt + n_dec], dec_seq, axis=0).reshape(ns, 6, d)
        w_in_b = w_in[l].astype(BF16)
        w_out_b = w_out[l].astype(BF16)
        cw = _pad_rows(conv_w[l], SUBLANES)
        lamp = jnp.zeros((SUBLANES, LANES), F32)
        lamp = lamp.at[0, :QK_DIM].set(lambda_q1[l]).at[1, :QK_DIM].set(lambda_k1[l])
        lamp = lamp.at[2, :QK_DIM].set(lambda_q2[l]).at[3, :QK_DIM].set(lambda_k2[l])
        lamp = lamp.at[4, :].set(lam_init)
        lng = chunk_ln_g[l].reshape(1, CHUNK_W)
        lnb = chunk_ln_b[l].reshape(1, CHUNK_W)
        ws = chunk_ws[l]
        bs = chunk_bs[l]
        wcat_p = jnp.concatenate([jnp.tril(ws[g]) for g in range(CHUNK_GROUPS)], axis=1).astype(BF16)
        bsf_p = jnp.repeat(bs.T, CHUNK_GW, axis=1)
        eye = jnp.eye(CHUNK // dec_seq, dtype=F32)
        wcat_s = jnp.concatenate([jnp.kron(eye, jnp.tril(ws[g, :dec_seq, :dec_seq]))
                                  for g in range(CHUNK_GROUPS)], axis=1).astype(BF16)
        bsf_s = jnp.tile(jnp.repeat(bs[:, :dec_seq].T, CHUNK_GW, axis=1), (CHUNK // dec_seq, 1))
        g_sub = subln_g[l]
        wr_pad = jnp.concatenate([w_router[l], jnp.zeros((d, LANES - N_EXPERTS), F32)], axis=1)
        br_pad = jnp.concatenate([b_router[l], jnp.full((LANES - N_EXPERTS,), NEG_BIG, F32)]).reshape(1, LANES)
        l1g, l1b = ln1_g[l].reshape(1, d), ln1_b[l].reshape(1, d)
        l2g, l2b = ln2_g[l].reshape(1, d), ln2_b[l].reshape(1, d)

        (q_p, kf_p, kb_p, vf_p, vb_p, co_p, cho_p, cst_p, chst_p) = _inproj_prompt(
            y_p, mp[:, 1], mp[:, 0], w_in_b, cos_p, sin_p, cw, conv0, lng, lnb, wcat_p, bsf_p)
        att_p = _attn_prompt(q_p, kb_p, vb_p, lamp, jnp.tile(g_sub, 2).reshape(1, LANES))
        zero_carry = jnp.zeros((SUBLANES, LANES), F32)
        x1_p, h2_p, route_p, gate_p, cnt_p = _outproj(
            y_p, co_p, att_p, cho_p, mp[:, 2], mp[:, 4], mp[:, 3], w_out_b, l1g, l1b, wr_pad, br_pad,
            zero_carry, True)

        st = state_conv[l]
        prev1 = jnp.repeat(st[:, 1], dec_seq, axis=0)
        prev2 = jnp.where((tok_pos == 0)[:, None], jnp.repeat(st[:, 0], dec_seq, axis=0), prev1)
        (q_s, kf_s, kb_s, vf_s, vb_s, co_s, cho_s, z_s, vn_s) = _inproj_sample(
            y_s, ms[:, 1], ms[:, 0], w_in_b, cos_s, sin_s, cw, prev1, prev2, lng, lnb, wcat_s, bsf_s, dec_seq)
        q3 = q_s.reshape(n_dec, dec_seq, -1)
        qbd = jnp.where(qbd_mask[None], q3[:, r_q, :], jnp.zeros((), BF16))
        pad3 = lambda a: jnp.concatenate(
            [a.reshape(n_dec, dec_seq, -1),
             jnp.zeros((n_dec, NEW_PAD - dec_seq, a.shape[-1]), a.dtype)], axis=1)
        att_s = _attn_sample(layer, page_table, cache_k4, cache_v4, qbd, pad3(kf_s), pad3(vf_s), lamp,
                             jnp.tile(g_sub, ATT_HEADS).reshape(1, ATT_W), dec_seq)
        x1_s, h2_s, route_s, gate_s, cnt_all = _outproj(
            y_s, co_s, att_s.reshape(ns, ATT_W).astype(BF16), cho_s, ms[:, 2], ms[:, 4], ms[:, 3],
            w_out_b, l1g, l1b, wr_pad, br_pad, cnt_p, False)

        counts = cnt_all[0, :N_EXPERTS].astype(jnp.int32)
        pcounts = ((counts + BM_EXPERT - 1) // BM_EXPERT) * BM_EXPERT
        pends = jnp.cumsum(pcounts)
        pstart = (pends - pcounts).astype(jnp.int32)
        blk_row = jnp.arange(n_blocks, dtype=jnp.int32) * BM_EXPERT
        block_e = jnp.minimum(jnp.sum(blk_row[:, None] >= pends[None, :], axis=1), N_EXPERTS - 1).astype(jnp.int32)
        n_used = (pends[-1] // BM_EXPERT).astype(jnp.int32).reshape(1)
        route8_p = route_p.reshape(np_tok, LANES)[:, :2 * TOP_K]
        route8_s = route_s[:, :2 * TOP_K]
        xs = jnp.zeros((n_rows, d), F32)
        xs = _dispatch(pstart, route8_p, h2_p.reshape(np_tok, d), xs)
        xs = _dispatch(pstart, route8_s, h2_s, xs)
        yb = _experts(layer, block_e, n_used, xs, w_gate, b_gate, w_up, b_up, w_down, b_down)
        y_p = _combine(pstart, route8_p, gate_p.reshape(np_tok, LANES), x1_p.reshape(np_tok, d),
                       mod[:n_prompt, 5 * d:], l2g, l2b, yb, seq).reshape(n_prompt, seq, d)
        y_s = _combine(pstart, route8_s, gate_s, x1_s, ms[:, 5], l2g, l2b, yb, 1)

        outs["kp"].append(kf_p.reshape(n_prompt, seq, ATT_HEADS, 2, QK_DIM))
        outs["vp"].append(vf_p.reshape(n_prompt, seq, ATT_HEADS, V_DIM))
        outs["cp"].append(cst_p)
        outs["chp"].append(chst_p)
        outs["ks"].append(kf_s.reshape(n_dec, dec_seq, ATT_HEADS, 2, QK_DIM))
        outs["vs"].append(vf_s.reshape(n_dec, dec_seq, ATT_HEADS, V_DIM))
        outs["cs"].append(z_s.reshape(n_dec, dec_seq, CONV_W)[:, dec_seq - (CONV_K - 1):])
        outs["chs"].append(vn_s.reshape(n_dec, dec_seq, CHUNK_W))

    st_ = lambda k: jnp.stack(outs[k])
    return (y_p, y_s.reshape(n_dec, dec_seq, d), st_("kp"), st_("vp"), st_("cp"), st_("chp"),
            st_("ks"), st_("vs"), st_("cs"), st_("chs"))
```

```python
import functools
import math

import jax
import jax.numpy as jnp
from jax import lax
from jax.experimental import pallas as pl
from jax.experimental.pallas import tpu as pltpu

F32 = jnp.float32
BF16 = jnp.bfloat16

D_MODEL = 1024
CONV_W = 256
CONV_K = 3
ATT_HEADS = 8
QK_DIM = 32
V_DIM = 64
ATT_W = ATT_HEADS * V_DIM
CHUNK_GROUPS = 4
CHUNK_GW = 64
CHUNK_W = CHUNK_GROUPS * CHUNK_GW
CHUNK = 128
O_CH = 0
O_Q = 3 * CONV_W
O_K = O_Q + ATT_HEADS * 2 * QK_DIM
O_V = O_K + ATT_HEADS * 2 * QK_DIM
O_U = O_V + ATT_W
O_SV = O_U + CHUNK_W
IN_W = O_SV + CHUNK_W
N_EXPERTS = 32
TOP_K = 4
D_FF = 1024
SWIGLU_ALPHA = 1.702
SWIGLU_LIMIT = 7.0
PAGE_SIZE = 128
ROPE_THETA = 10000.0
LN_EPS = 1e-5
DEPTH = 4
DN_ALPHA = (2 * DEPTH) ** 0.25
Q_SCALE = QK_DIM ** -0.5 * math.log2(math.e)

LANES = 128
SUBLANES = 8
ROW_TILES = D_MODEL // LANES
VMEM_BYTES_V7X = 64 * 1024 * 1024

TM_PROJ = 512
TK_ATT = 512
PAGES_PER_STEP = 8
BM_EXPERT = 256
TD_ROWS = 256
NEW_PAD = PAGE_SIZE
NEG_BIG = -1e30
VMEM_LIMIT = 56 * 1024 * 1024


def _cparams(sem):
    return pltpu.CompilerParams(dimension_semantics=sem, vmem_limit_bytes=VMEM_LIMIT)


def _split_bf16(a):
    hi = a.astype(BF16)
    lo = (a - hi.astype(F32)).astype(BF16)
    return hi, lo


def _dot(a, b):
    return jnp.dot(a, b, preferred_element_type=F32)


def _dot_split_lhs(a, b_bf16):
    hi, lo = _split_bf16(a)
    return _dot(hi, b_bf16) + _dot(lo, b_bf16)


def _dot3(a, b):
    ah, al = _split_bf16(a)
    bh, bl = _split_bf16(b)
    return _dot(ah, bh) + (_dot(al, bh) + _dot(ah, bl))


def _store_rows_as_tiles(ref, val):
    for j in range(ROW_TILES):
        ref[:, j, :] = val[:, j * LANES:(j + 1) * LANES]


def _load_rows_from_tiles(ref):
    return jnp.concatenate([ref[:, j, :] for j in range(ROW_TILES)], axis=1)


def _layer_norm_rows(v, g, b):
    mu = jnp.mean(v, axis=-1, keepdims=True)
    d = v - mu
    var = jnp.mean(d * d, axis=-1, keepdims=True)
    return d * lax.rsqrt(var + LN_EPS) * g + b


def _ada_kernel(c_ref, w_ref, b_ref, o_ref):
    o_ref[0] = _dot3(c_ref[...], w_ref[0]) + b_ref[0]


def _ada_call(c_all, w_ada, b_ada):
    depth, d, n6 = w_ada.shape
    m = c_all.shape[0]
    tn = 1024
    return pl.pallas_call(
        _ada_kernel,
        grid=(depth, n6 // tn),
        in_specs=[
            pl.BlockSpec((m, d), lambda l, j: (0, 0)),
            pl.BlockSpec((1, d, tn), lambda l, j: (l, 0, j)),
            pl.BlockSpec((1, 1, tn), lambda l, j: (l, 0, j)),
        ],
        out_specs=pl.BlockSpec((1, m, tn), lambda l, j: (l, 0, j)),
        out_shape=jax.ShapeDtypeStruct((depth, m, n6), F32),
        compiler_params=_cparams(("arbitrary", "arbitrary")),
        name="ada",
    )(c_all, w_ada, b_ada.reshape(depth, 1, n6))


def _rope_chunks(p, cos_t, sin_t, scale):
    lane = lax.broadcasted_iota(jnp.int32, (1, LANES), 1)
    first_half = (lane % QK_DIM) < (QK_DIM // 2)
    outs = []
    for c in range(p.shape[1] // LANES):
        xc = p[:, c * LANES:(c + 1) * LANES]
        up = pltpu.roll(xc, LANES - QK_DIM // 2, 1)
        dn = pltpu.roll(xc, QK_DIM // 2, 1)
        r = xc * cos_t + jnp.where(first_half, up, dn) * sin_t
        if scale is not None:
            r = r * scale
        outs.append(r)
    return jnp.concatenate(outs, axis=1)


def _group_norm(sv, g, b):
    ri = lax.broadcasted_iota(jnp.int32, (CHUNK_W, CHUNK_W), 0) // CHUNK_GW
    ci = lax.broadcasted_iota(jnp.int32, (CHUNK_W, CHUNK_W), 1) // CHUNK_GW
    avg = jnp.where(ri == ci, 1.0 / CHUNK_GW, 0.0).astype(BF16)
    mu = _dot_split_lhs(sv, avg)
    d = sv - mu
    var = _dot_split_lhs(d * d, avg)
    return d * lax.rsqrt(var + LN_EPS) * g + b


def _chunk_mix(vn, wcat_ref, bsf_ref):
    gid = lax.broadcasted_iota(jnp.int32, (1, CHUNK_W), 1) // CHUNK_GW
    vb = vn.astype(BF16)
    zero = jnp.zeros_like(vb[:CHUNK])
    outs = []
    for c in range(vn.shape[0] // CHUNK):
        vc = vb[c * CHUNK:(c + 1) * CHUNK]
        stack = jnp.concatenate([jnp.where(gid == g, vc, zero) for g in range(CHUNK_GROUPS)], axis=0)
        outs.append(_dot(wcat_ref[...], stack) + bsf_ref[...])
    return jnp.concatenate(outs, axis=0)


def _inproj_kernel(*refs, seq_mode, dec_seq):
    if seq_mode:
        (x_ref, sc_ref, sh_ref, w_ref, cos_ref, sin_ref, cw_ref, cprev_ref, lng_ref, lnb_ref,
         wcat_ref, bsf_ref,
         q_ref, kf_ref, kb_ref, vf_ref, vb_ref, co_ref, cho_ref, cst_ref, chst_ref, zc_ref) = refs
        x = x_ref[0]
        sc = sc_ref[0]
        sh = sh_ref[0]
    else:
        (x_ref, sc_ref, sh_ref, w_ref, cos_ref, sin_ref, cw_ref, p1_ref, p2_ref, lng_ref, lnb_ref,
         wcat_ref, bsf_ref,
         q_ref, kf_ref, kb_ref, vf_ref, vb_ref, co_ref, cho_ref, cst_ref, chst_ref) = refs
        x = x_ref[...]
        sc = sc_ref[...]
        sh = sh_ref[...]
    tm = x.shape[0]
    h = (x * (1.0 + sc) + sh).astype(BF16)

    pc = _dot(h, w_ref[:, O_CH:O_Q])
    conv_in, gate_b, gate_c = pc[:, :CONV_W], pc[:, CONV_W:2 * CONV_W], pc[:, 2 * CONV_W:]
    z = gate_c * conv_in
    row = lax.broadcasted_iota(jnp.int32, (tm, 1), 0)
    z1 = pltpu.roll(z, 1, 0)
    z2 = pltpu.roll(z, 2, 0)
    if seq_mode:
        @pl.when(pl.program_id(1) == 0)
        def _():
            zc_ref[0:2, :] = cprev_ref[0]
        c0 = zc_ref[0:1, :]
        c1 = zc_ref[1:2, :]
        z1 = jnp.where(row == 0, c1, z1)
        z2 = jnp.where(row == 0, c0, jnp.where(row == 1, c1, z2))
        zc_ref[0:2, :] = z[tm - 2:tm, :]
        cst_ref[0] = z[tm - 2:tm, :]
    else:
        pos_in_seq = row % dec_seq
        z1 = jnp.where(pos_in_seq >= 1, z1, p1_ref[...])
        z2 = jnp.where(pos_in_seq >= 2, z2, p2_ref[...])
        cst_ref[...] = z
    y = cw_ref[0:1, :] * z2 + cw_ref[1:2, :] * z1 + cw_ref[2:3, :] * z
    conv_out = (gate_b * y).astype(BF16)

    pqk = _dot(h, w_ref[:, O_Q:O_V])
    cos_t = cos_ref[...]
    sin_t = sin_ref[...]
    nq = O_K - O_Q
    qr = _rope_chunks(pqk[:, :nq], cos_t, sin_t, Q_SCALE)
    kr = _rope_chunks(pqk[:, nq:], cos_t, sin_t, None)
    pv = _dot(h, w_ref[:, O_V:O_U])

    pu = _dot(h, w_ref[:, O_U:IN_W])
    u, sv = pu[:, :CHUNK_W], pu[:, CHUNK_W:]
    vn = _group_norm(sv, lng_ref[...], lnb_ref[...])
    chunk_out = (u * _chunk_mix(vn, wcat_ref, bsf_ref)).astype(BF16)

    if seq_mode:
        q_ref[0] = qr.T.astype(BF16)
        kf_ref[0] = kr.T
        kb_ref[0] = kr.astype(BF16)
        vt = pv.T
        vf_ref[0] = vt
        vtb = vt.astype(BF16)
        for c in range(tm // TK_ATT):
            vb_ref[0, c] = vtb[:, c * TK_ATT:(c + 1) * TK_ATT]
        co_ref[0] = conv_out
        cho_ref[0] = chunk_out
        chst_ref[0] = vn[tm - CHUNK:tm, :]
    else:
        q_ref[...] = qr.astype(BF16)
        kf_ref[...] = kr
        kb_ref[...] = kr.astype(BF16)
        vf_ref[...] = pv
        vb_ref[...] = pv.astype(BF16)
        co_ref[...] = conv_out
        cho_ref[...] = chunk_out
        chst_ref[...] = vn


def _inproj_prompt(x, sc1, sh1, w_in_b, cos_t, sin_t, conv_w, conv_prev, ln_g, ln_b, wcat, bsf):
    b_, t_, d = x.shape
    tm = min(TM_PROJ, t_)
    nt = t_ // tm
    const2 = lambda b, i: (0, 0)
    tok3 = lambda b, i: (b, i, 0)
    tokT = lambda b, i: (b, 0, i)
    bat3 = lambda b, i: (b, 0, 0)
    qk_w = ATT_HEADS * 2 * QK_DIM
    outs = pl.pallas_call(
        functools.partial(_inproj_kernel, seq_mode=True, dec_seq=0),
        grid=(b_, nt),
        in_specs=[
            pl.BlockSpec((1, tm, d), tok3),
            pl.BlockSpec((1, 1, d), bat3),
            pl.BlockSpec((1, 1, d), bat3),
            pl.BlockSpec((d, IN_W), const2),
            pl.BlockSpec((tm, LANES), lambda b, i: (i, 0)),
            pl.BlockSpec((tm, LANES), lambda b, i: (i, 0)),
            pl.BlockSpec((SUBLANES, CONV_W), const2),
            pl.BlockSpec((1, CONV_K - 1, CONV_W), bat3),
            pl.BlockSpec((1, CHUNK_W), const2),
            pl.BlockSpec((1, CHUNK_W), const2),
            pl.BlockSpec((CHUNK, CHUNK_GROUPS * CHUNK), const2),
            pl.BlockSpec((CHUNK, CHUNK_W), const2),
        ],
        out_specs=[
            pl.BlockSpec((1, qk_w, tm), tokT),
            pl.BlockSpec((1, qk_w, tm), tokT),
            pl.BlockSpec((1, tm, qk_w), tok3),
            pl.BlockSpec((1, ATT_W, tm), tokT),
            pl.BlockSpec((1, tm // TK_ATT, ATT_W, TK_ATT), lambda b, i: (b, i, 0, 0)),
            pl.BlockSpec((1, tm, CONV_W), tok3),
            pl.BlockSpec((1, tm, CHUNK_W), tok3),
            pl.BlockSpec((1, CONV_K - 1, CONV_W), bat3),
            pl.BlockSpec((1, CHUNK, CHUNK_W), bat3),
        ],
        out_shape=[
            jax.ShapeDtypeStruct((b_, qk_w, t_), BF16),
            jax.ShapeDtypeStruct((b_, qk_w, t_), F32),
            jax.ShapeDtypeStruct((b_, t_, qk_w), BF16),
            jax.ShapeDtypeStruct((b_, ATT_W, t_), F32),
            jax.ShapeDtypeStruct((b_, t_ // TK_ATT, ATT_W, TK_ATT), BF16),
            jax.ShapeDtypeStruct((b_, t_, CONV_W), BF16),
            jax.ShapeDtypeStruct((b_, t_, CHUNK_W), BF16),
            jax.ShapeDtypeStruct((b_, CONV_K - 1, CONV_W), F32),
            jax.ShapeDtypeStruct((b_, CHUNK, CHUNK_W), F32),
        ],
        scratch_shapes=[pltpu.VMEM((SUBLANES, CONV_W), F32)],
        compiler_params=_cparams(("arbitrary", "arbitrary")),
        name="inproj_prompt",
    )(x, sc1, sh1, w_in_b, cos_t, sin_t, conv_w, conv_prev, ln_g, ln_b, wcat, bsf)
    return outs


def _inproj_sample(x2d, sc1, sh1, w_in_b, cos_t, sin_t, conv_w, prev1, prev2, ln_g, ln_b, wcat, bsf,
                   dec_seq):
    n, d = x2d.shape
    qk_w = ATT_HEADS * 2 * QK_DIM
    full = lambda shape: pl.BlockSpec(shape, lambda i: tuple(0 for _ in shape))
    outs = pl.pallas_call(
        functools.partial(_inproj_kernel, seq_mode=False, dec_seq=dec_seq),
        grid=(1,),
        in_specs=[
            full((n, d)), full((n, d)), full((n, d)), full((d, IN_W)),
            full((n, LANES)), full((n, LANES)), full((SUBLANES, CONV_W)),
            full((n, CONV_W)), full((n, CONV_W)), full((1, CHUNK_W)), full((1, CHUNK_W)),
            full((CHUNK, CHUNK_GROUPS * CHUNK)), full((CHUNK, CHUNK_W)),
        ],
        out_specs=[
            full((n, qk_w)), full((n, qk_w)), full((n, qk_w)), full((n, ATT_W)), full((n, ATT_W)),
            full((n, CONV_W)), full((n, CHUNK_W)), full((n, CONV_W)), full((n, CHUNK_W)),
        ],
        out_shape=[
            jax.ShapeDtypeStruct((n, qk_w), BF16),
            jax.ShapeDtypeStruct((n, qk_w), F32),
            jax.ShapeDtypeStruct((n, qk_w), BF16),
            jax.ShapeDtypeStruct((n, ATT_W), F32),
            jax.ShapeDtypeStruct((n, ATT_W), BF16),
            jax.ShapeDtypeStruct((n, CONV_W), BF16),
            jax.ShapeDtypeStruct((n, CHUNK_W), BF16),
            jax.ShapeDtypeStruct((n, CONV_W), F32),
            jax.ShapeDtypeStruct((n, CHUNK_W), F32),
        ],
        compiler_params=_cparams(("arbitrary",)),
        name="inproj_sample",
    )(x2d, sc1, sh1, w_in_b, cos_t, sin_t, conv_w, prev1, prev2, ln_g, ln_b, wcat, bsf)
    return outs


def _lambda_from(lam_ref):
    lp = lam_ref[...]
    s1 = jnp.sum(lp[0:1, :] * lp[1:2, :], axis=1, keepdims=True)
    s2 = jnp.sum(lp[2:3, :] * lp[3:4, :], axis=1, keepdims=True)
    lam_init = lp[4:5, 0:1]
    return jnp.exp(s1) - jnp.exp(s2) + lam_init, lam_init


L_ROWS = 16


def _attn_prompt_kernel(qt_ref, k_ref, vt_ref, lam_ref, g_ref, o_ref, qm_ref, m_ref, acc_ref):
    tq = qt_ref.shape[2]
    tk = vt_ref.shape[3]
    qi = pl.program_id(2)
    n_hm = 2 * 2

    qt = qt_ref[0]
    row_hm = lax.broadcasted_iota(jnp.int32, (LANES, 1), 0) // QK_DIM
    zero = jnp.zeros_like(qt)
    for hm in range(n_hm):
        qm_ref[hm] = jnp.where(row_hm == hm, qt, zero)
    m_ref[...] = jnp.full(m_ref.shape, NEG_BIG, F32)
    acc_ref[...] = jnp.zeros(acc_ref.shape, F32)

    def step(ki, masked):
        start = pl.multiple_of(ki * tk, tk)
        kt = k_ref[0, pl.ds(start, tk), :]
        vt = vt_ref[0, ki]
        ones = jnp.ones((L_ROWS, tk), BF16)
        if masked:
            key = lax.broadcasted_iota(jnp.int32, (tk, tq), 0)
            qry = lax.broadcasted_iota(jnp.int32, (tk, tq), 1)
            valid = key <= qry
        m_old = [m_ref[hm] for hm in range(n_hm)]
        acc_old = [acc_ref[hm] for hm in range(n_hm)]
        lhs = [jnp.concatenate([vt[h * V_DIM:(h + 1) * V_DIM, :], ones], axis=0) for h in range(2)]
        s = [_dot(kt, qm_ref[hm]) for hm in range(n_hm)]
        if masked:
            s = [jnp.where(valid, s_, NEG_BIG) for s_ in s]
        m_new = [jnp.maximum(m_old[hm], jnp.max(s[hm], axis=0, keepdims=True)) for hm in range(n_hm)]
        p = [jnp.exp2(s[hm] - m_new[hm]).astype(BF16) for hm in range(n_hm)]
        pv = [_dot(lhs[hm // 2], p[hm]) for hm in range(n_hm)]
        for hm in range(n_hm):
            acc_ref[hm] = jnp.exp2(m_old[hm] - m_new[hm]) * acc_old[hm] + pv[hm]
            m_ref[hm] = m_new[hm]

    def body(ki, carry):
        step(ki, False)
        return carry

    lax.fori_loop(0, qi, body, 0)
    step(qi, True)

    lam, lam_init = _lambda_from(lam_ref)
    halves = []
    for h in range(2):
        a0 = acc_ref[2 * h]
        a1 = acc_ref[2 * h + 1]
        o0 = a0[:V_DIM] * (1.0 / a0[V_DIM:V_DIM + 1])
        o1 = a1[:V_DIM] * (1.0 / a1[V_DIM:V_DIM + 1])
        d = o0 - lam * o1
        ms = jnp.sum(d * d, axis=0, keepdims=True) * (1.0 / V_DIM)
        halves.append(d * lax.rsqrt(ms + LN_EPS) * g_ref[...] * (1.0 - lam_init))
    o_ref[0] = jnp.concatenate(halves, axis=0).T.astype(o_ref.dtype)


def _attn_prompt(qt, k, vt4, lamp, g_col):
    b_, t_, _ = k.shape
    tk = vt4.shape[3]
    tq = tk
    nq = t_ // tq
    npair = ATT_HEADS // 2
    return pl.pallas_call(
        _attn_prompt_kernel,
        grid=(b_, npair, nq),
        in_specs=[
            pl.BlockSpec((1, LANES, tq), lambda b, h, i: (b, h, i)),
            pl.BlockSpec((1, t_, LANES), lambda b, h, i: (b, 0, h)),
            pl.BlockSpec((1, t_ // tk, LANES, tk), lambda b, h, i: (b, 0, h, 0)),
            pl.BlockSpec((SUBLANES, LANES), lambda b, h, i: (0, 0)),
            pl.BlockSpec((V_DIM, tq), lambda b, h, i: (0, 0)),
        ],
        out_specs=pl.BlockSpec((1, tq, LANES), lambda b, h, i: (b, i, h)),
        out_shape=jax.ShapeDtypeStruct((b_, t_, ATT_W), BF16),
        scratch_shapes=[
            pltpu.VMEM((4, LANES, tq), BF16),
            pltpu.VMEM((4, 1, tq), F32),
            pltpu.VMEM((4, V_DIM + L_ROWS, tq), F32),
        ],
        compiler_params=_cparams(("arbitrary", "arbitrary", "arbitrary")),
        name="attn_prompt",
    )(qt, k, vt4, lamp, g_col)


def _attn_sample_kernel(l_ref, pt_ref, *refs, pps, dec_seq):
    k_refs = refs[:pps]
    v_refs = refs[pps:2 * pps]
    qbd_ref, kn_ref, vn_ref, lam_ref, g_ref, o_ref, m_ref, l_acc_ref, acc_ref = refs[2 * pps:]
    j = pl.program_id(1)
    nrow = qbd_ref.shape[1]
    half = nrow // 2

    @pl.when(j == 0)
    def _():
        m_ref[...] = jnp.full(m_ref.shape, NEG_BIG, F32)
        l_acc_ref[...] = jnp.zeros(l_acc_ref.shape, F32)
        acc_ref[...] = jnp.zeros(acc_ref.shape, F32)

    qbd = qbd_ref[0]

    nt = (((1,), (1,)), ((), ()))

    def online(s, vals, vals_transposed):
        m_old = m_ref[...]
        m_new = jnp.maximum(m_old, jnp.max(s, axis=1, keepdims=True))
        alpha = jnp.exp2(m_old - m_new)
        p = jnp.exp2(s - m_new)
        m_ref[...] = m_new
        l_acc_ref[...] = alpha * l_acc_ref[...] + jnp.sum(p, axis=1, keepdims=True)
        pb = p.astype(BF16)
        pv = None
        for r, val in enumerate(vals):
            pr = pb[:, r * PAGE_SIZE:(r + 1) * PAGE_SIZE]
            if vals_transposed:
                t = lax.dot_general(pr, val, nt, preferred_element_type=F32)
            else:
                t = _dot(pr, val)
            pv = t if pv is None else pv + t
        acc_ref[...] = alpha * acc_ref[...] + pv

    s_parts = [_dot(qbd, k_refs[r][0, 0].astype(BF16)) for r in range(pps)]
    online(jnp.concatenate(s_parts, axis=1), [v_refs[r][0, 0].astype(BF16) for r in range(pps)], True)

    @pl.when(j == pl.num_programs(1) - 1)
    def _():
        kn = kn_ref[0].astype(BF16)
        s = lax.dot_general(qbd, kn, nt, preferred_element_type=F32)
        r_ = lax.broadcasted_iota(jnp.int32, s.shape, 0)
        c_ = lax.broadcasted_iota(jnp.int32, s.shape, 1)
        qpos = (r_ % half) // ATT_HEADS
        s = jnp.where(c_ <= qpos, s, NEG_BIG)
        online(s, [vn_ref[0].astype(BF16)], False)

        lam, lam_init = _lambda_from(lam_ref)
        o = acc_ref[...] / l_acc_ref[...]
        d = o[:half] - lam * o[half:]
        rr = lax.broadcasted_iota(jnp.int32, d.shape, 0) % ATT_HEADS
        cc = lax.broadcasted_iota(jnp.int32, d.shape, 1) // V_DIM
        d = jnp.where(rr == cc, d, 0.0)
        ms = jnp.sum(d * d, axis=1, keepdims=True) * (1.0 / V_DIM)
        dn = d * lax.rsqrt(ms + LN_EPS) * g_ref[...] * (1.0 - lam_init)
        o_ref[0] = jnp.sum(dn.reshape(dec_seq, ATT_HEADS, ATT_W), axis=1)


def _attn_sample(layer, page_table, cache_k4, cache_v4, qbd, kn_pad, vn_pad, lamp, g8, dec_seq):
    n_dec, n_pages = page_table.shape
    pps = PAGES_PER_STEP
    while n_pages % pps:
        pps //= 2
    nrow = qbd.shape[1]
    qkw = qbd.shape[2]

    def page_map(r):
        return lambda b, j, l_ref, pt_ref: (l_ref[0], pt_ref[b * n_pages + j * pps + r], 0, 0)

    seq3 = lambda b, j, l_ref, pt_ref: (b, 0, 0)
    const2 = lambda b, j, l_ref, pt_ref: (0, 0)
    in_specs = ([pl.BlockSpec((1, 1, qkw, PAGE_SIZE), page_map(r)) for r in range(pps)]
                + [pl.BlockSpec((1, 1, ATT_W, PAGE_SIZE), page_map(r)) for r in range(pps)]
                + [pl.BlockSpec((1, nrow, qkw), seq3),
                   pl.BlockSpec((1, NEW_PAD, qkw), seq3),
                   pl.BlockSpec((1, NEW_PAD, ATT_W), seq3),
                   pl.BlockSpec((SUBLANES, LANES), const2),
                   pl.BlockSpec((1, ATT_W), const2)])
    grid_spec = pltpu.PrefetchScalarGridSpec(
        num_scalar_prefetch=2,
        grid=(n_dec, n_pages // pps),
        in_specs=in_specs,
        out_specs=pl.BlockSpec((1, dec_seq, ATT_W), seq3),
        scratch_shapes=[
            pltpu.VMEM((nrow, 1), F32),
            pltpu.VMEM((nrow, 1), F32),
            pltpu.VMEM((nrow, ATT_W), F32),
        ],
    )
    return pl.pallas_call(
        functools.partial(_attn_sample_kernel, pps=pps, dec_seq=dec_seq),
        grid_spec=grid_spec,
        out_shape=jax.ShapeDtypeStruct((n_dec, dec_seq, ATT_W), F32),
        compiler_params=_cparams(("arbitrary", "arbitrary")),
        name="attn_sample",
    )(layer, page_table.reshape(-1), *([cache_k4] * pps), *([cache_v4] * pps),
      qbd, kn_pad, vn_pad, lamp, g8)


def _outproj_kernel(x_ref, co_ref, att_ref, cho_ref, g1_ref, sc2_ref, sh2_ref, wo_ref, lg_ref, lb_ref,
                    wr_ref, br_ref, cin_ref,
                    x1_ref, h2_ref, route_ref, gate_ref, cnt_ref, carry_ref, *, batched):
    first = (pl.program_id(0) == 0) & (pl.program_id(1) == 0)

    @pl.when(first)
    def _():
        carry_ref[...] = cin_ref[...]

    if batched:
        x, co, att, cho = x_ref[0], co_ref[0], att_ref[0], cho_ref[0]
        g1, sc2, sh2 = g1_ref[0], sc2_ref[0], sh2_ref[0]
    else:
        x, co, att, cho = x_ref[...], co_ref[...], att_ref[...], cho_ref[...]
        g1, sc2, sh2 = g1_ref[...], sc2_ref[...], sh2_ref[...]
    tm = x.shape[0]
    mix = jnp.concatenate([co, att, cho], axis=1)
    y = _dot(mix, wo_ref[...])
    x1 = _layer_norm_rows(DN_ALPHA * x + (1.0 + g1) * y, lg_ref[...], lb_ref[...])
    h2 = x1 * (1.0 + sc2) + sh2

    logits = _dot3(h2, wr_ref[...]) + br_ref[...]
    lane = lax.broadcasted_iota(jnp.int32, (1, LANES), 1)
    lane_f = lane.astype(F32)
    work = logits
    vals, hots, idxs = [], [], []
    for _ in range(TOP_K):
        mx = jnp.max(work, axis=1, keepdims=True)
        idx = jnp.min(jnp.where(work == mx, lane_f, float(LANES)), axis=1, keepdims=True)
        hot = lane_f == idx
        work = jnp.where(hot, -3e38, work)
        vals.append(mx)
        hots.append(hot)
        idxs.append(idx)
    exps = [jnp.exp(v - vals[0]) for v in vals]
    inv = 1.0 / (exps[0] + exps[1] + exps[2] + exps[3])

    any_hot = hots[0] | hots[1] | hots[2] | hots[3]
    a_b = jnp.where(any_hot, 1.0, 0.0).astype(BF16)
    ri = lax.broadcasted_iota(jnp.int32, (tm, tm), 0)
    ci = lax.broadcasted_iota(jnp.int32, (tm, tm), 1)
    stril = jnp.where(ci < ri, 1.0, 0.0).astype(BF16)
    before = _dot(stril, a_b) + carry_ref[0:1, :]
    new_carry = carry_ref[0:1, :] + jnp.sum(a_b.astype(F32), axis=0, keepdims=True)
    carry_ref[0:1, :] = new_carry

    route = jnp.zeros((tm, LANES), F32)
    gates = jnp.zeros((tm, LANES), F32)
    for k in range(TOP_K):
        rank_k = jnp.sum(jnp.where(hots[k], before, 0.0), axis=1, keepdims=True)
        route = jnp.where(lane == k, idxs[k], route)
        route = jnp.where(lane == TOP_K + k, rank_k, route)
        gates = jnp.where(lane == k, exps[k] * inv, gates)

    if batched:
        x1_ref[0] = x1
        _store_rows_as_tiles(h2_ref.at[0], h2)
        route_ref[0] = route.astype(jnp.int32)
        gate_ref[0] = gates
    else:
        x1_ref[...] = x1
        _store_rows_as_tiles(h2_ref, h2)
        route_ref[...] = route.astype(jnp.int32)
        gate_ref[...] = gates
    cnt_ref[...] = jnp.broadcast_to(new_carry, cnt_ref.shape)


def _outproj(x, co, att, cho, g1, sc2, sh2, w_out_b, ln_g, ln_b, wr_pad, br_pad, carry_in, batched):
    if batched:
        b_, t_, d = x.shape
        tm = min(TM_PROJ, t_)
        grid = (b_, t_ // tm)
        tok = lambda w: pl.BlockSpec((1, tm, w), lambda b, i: (b, i, 0))
        mod = pl.BlockSpec((1, 1, d), lambda b, i: (b, 0, 0))
        oshape = lambda w, dt: jax.ShapeDtypeStruct((b_, t_, w), dt)
        h2_spec = pl.BlockSpec((1, tm, ROW_TILES, LANES), lambda b, i: (b, i, 0, 0))
        h2_shape = jax.ShapeDtypeStruct((b_, t_, ROW_TILES, LANES), F32)
    else:
        n, d = x.shape
        tm = n
        grid = (1, 1)
        tok = lambda w: pl.BlockSpec((n, w), lambda b, i: (0, 0))
        mod = tok(d)
        oshape = lambda w, dt: jax.ShapeDtypeStruct((n, w), dt)
        h2_spec = pl.BlockSpec((n, ROW_TILES, LANES), lambda b, i: (0, 0, 0))
        h2_shape = jax.ShapeDtypeStruct((n, ROW_TILES, LANES), F32)
    const2 = lambda b, i: (0, 0)
    return pl.pallas_call(
        functools.partial(_outproj_kernel, batched=batched),
        grid=grid,
        in_specs=[
            tok(d), tok(CONV_W), tok(ATT_W), tok(CHUNK_W), mod, mod, mod,
            pl.BlockSpec((d, d), const2),
            pl.BlockSpec((1, d), const2),
            pl.BlockSpec((1, d), const2),
            pl.BlockSpec((d, LANES), const2),
            pl.BlockSpec((1, LANES), const2),
            pl.BlockSpec((SUBLANES, LANES), const2),
        ],
        out_specs=[tok(d), h2_spec, tok(LANES), tok(LANES), pl.BlockSpec((SUBLANES, LANES), const2)],
        out_shape=[oshape(d, F32), h2_shape, oshape(LANES, jnp.int32), oshape(LANES, F32),
                   jax.ShapeDtypeStruct((SUBLANES, LANES), F32)],
        scratch_shapes=[pltpu.VMEM((SUBLANES, LANES), F32)],
        compiler_params=_cparams(("arbitrary", "arbitrary")),
        name="outproj_prompt" if batched else "outproj_sample",
    )(x, co, att, cho, g1, sc2, sh2, w_out_b, ln_g, ln_b, wr_pad, br_pad, carry_in)


def _dispatch_kernel(pstart_ref, route_ref, h_ref, xs_in_ref, xs_ref, sem):
    del xs_in_ref
    td = h_ref.shape[0]

    def row_copy(r, dst):
        return pltpu.make_async_copy(h_ref.at[pl.ds(r, 1)], xs_ref.at[pl.ds(dst, 1)], sem)

    def issue(r, carry):
        for k in range(TOP_K):
            e = route_ref[0, 0, r * 2 * TOP_K + k]
            rk = route_ref[0, 0, r * 2 * TOP_K + TOP_K + k]
            row_copy(r, pstart_ref[e] + rk).start()
        return carry

    lax.fori_loop(0, td, issue, 0, unroll=2)

    def drain(i, carry):
        row_copy(0, 0).wait()
        return carry

    lax.fori_loop(0, td * TOP_K, drain, 0, unroll=8)


def _dispatch(pstart, route8, h3d, xs):
    n = h3d.shape[0]
    td = min(TD_ROWS, n)
    nt = n // td
    route_t = route8.reshape(nt, 1, td * 2 * TOP_K)
    grid_spec = pltpu.PrefetchScalarGridSpec(
        num_scalar_prefetch=1,
        grid=(nt,),
        in_specs=[
            pl.BlockSpec((1, 1, td * 2 * TOP_K), lambda i, ps: (i, 0, 0), memory_space=pltpu.SMEM),
            pl.BlockSpec((td, ROW_TILES, LANES), lambda i, ps: (i, 0, 0)),
            pl.BlockSpec(memory_space=pl.ANY),
        ],
        out_specs=pl.BlockSpec(memory_space=pl.ANY),
        scratch_shapes=[pltpu.SemaphoreType.DMA(())],
    )
    return pl.pallas_call(
        _dispatch_kernel,
        grid_spec=grid_spec,
        out_shape=jax.ShapeDtypeStruct(xs.shape, xs.dtype),
        input_output_aliases={3: 0},
        compiler_params=_cparams(("arbitrary",)),
        name="moe_dispatch",
    )(pstart, route_t, h3d, xs)


def _expert_kernel(l_ref, be_ref, nu_ref, x_ref, wg_ref, bg_ref, wu_ref, bu_ref, wd_ref, bd_ref,
                   y_ref, wgb_ref, wub_ref, wdb_ref):
    b = pl.program_id(0)
    changed = (b == 0) | (be_ref[b] != be_ref[jnp.maximum(b - 1, 0)])

    @pl.when(changed)
    def _():
        wgb_ref[...] = wg_ref[0, 0].astype(BF16)
        wub_ref[...] = wu_ref[0, 0].astype(BF16)
        wdb_ref[...] = wd_ref[0, 0].astype(BF16)

    @pl.when(b < nu_ref[0])
    def _():
        x = _load_rows_from_tiles(x_ref).astype(BF16)
        g = jnp.minimum(_dot(x, wgb_ref[...]) + bg_ref[0, 0], SWIGLU_LIMIT)
        u = jnp.clip(_dot(x, wub_ref[...]) + bu_ref[0, 0], -SWIGLU_LIMIT, SWIGLU_LIMIT)
        hmid = (u + 1.0) * (g * jax.nn.sigmoid(g * SWIGLU_ALPHA))
        _store_rows_as_tiles(y_ref, _dot(hmid.astype(BF16), wdb_ref[...]) + bd_ref[0, 0])

    @pl.when(b >= nu_ref[0])
    def _():
        y_ref[...] = jnp.zeros(y_ref.shape, y_ref.dtype)


def _experts(layer, block_e, n_used, xs, w_gate, b_gate, w_up, b_up, w_down, b_down):
    n_rows = xs.shape[0]
    d = w_gate.shape[-2]
    bm = BM_EXPERT
    nb = n_rows // bm
    dff = w_gate.shape[-1]
    wmap = lambda b, l, be, nu: (l[0], be[b], 0, 0)
    grid_spec = pltpu.PrefetchScalarGridSpec(
        num_scalar_prefetch=3,
        grid=(nb,),
        in_specs=[
            pl.BlockSpec((bm, ROW_TILES, LANES), lambda b, l, be, nu: (b, 0, 0)),
            pl.BlockSpec((1, 1, d, dff), wmap),
            pl.BlockSpec((1, 1, 1, dff), wmap),
            pl.BlockSpec((1, 1, d, dff), wmap),
            pl.BlockSpec((1, 1, 1, dff), wmap),
            pl.BlockSpec((1, 1, dff, d), wmap),
            pl.BlockSpec((1, 1, 1, d), wmap),
        ],
        out_specs=pl.BlockSpec((bm, ROW_TILES, LANES), lambda b, l, be, nu: (b, 0, 0)),
        scratch_shapes=[pltpu.VMEM((d, dff), BF16), pltpu.VMEM((d, dff), BF16), pltpu.VMEM((dff, d), BF16)],
    )
    depth, ne = b_gate.shape[:2]
    return pl.pallas_call(
        _expert_kernel,
        grid_spec=grid_spec,
        out_shape=jax.ShapeDtypeStruct((n_rows, ROW_TILES, LANES), F32),
        compiler_params=_cparams(("arbitrary",)),
        name="moe_experts",
    )(layer, block_e, n_used, xs, w_gate, b_gate.reshape(depth, ne, 1, dff), w_up,
      b_up.reshape(depth, ne, 1, dff), w_down, b_down.reshape(depth, ne, 1, d))


def _combine_kernel(pstart_ref, route_ref, gate_ref, x1_ref, g2_ref, lg_ref, lb_ref, yb_ref,
                    o_ref, buf_ref, sem):
    tc = x1_ref.shape[0]

    def row_copy(src, k, r):
        return pltpu.make_async_copy(yb_ref.at[pl.ds(src, 1)], buf_ref.at[k, pl.ds(r, 1)], sem)

    def issue(r, carry):
        for k in range(TOP_K):
            e = route_ref[0, 0, r * 2 * TOP_K + k]
            rk = route_ref[0, 0, r * 2 * TOP_K + TOP_K + k]
            row_copy(pstart_ref[e] + rk, k, r).start()
        return carry

    lax.fori_loop(0, tc, issue, 0, unroll=2)

    def drain(i, carry):
        row_copy(0, 0, 0).wait()
        return carry

    lax.fori_loop(0, tc * TOP_K, drain, 0, unroll=8)

    gates = gate_ref[...]
    f = gates[:, 0:1] * _load_rows_from_tiles(buf_ref.at[0])
    for k in range(1, TOP_K):
        f = f + gates[:, k:k + 1] * _load_rows_from_tiles(buf_ref.at[k])
    o_ref[...] = _layer_norm_rows(DN_ALPHA * x1_ref[...] + (1.0 + g2_ref[...]) * f,
                                  lg_ref[...], lb_ref[...])


def _combine(pstart, route8, gates, x1_2d, g2, ln_g, ln_b, yb, rows_per_mod):
    n, d = x1_2d.shape
    tc = min(TD_ROWS, n)
    nt = n // tc
    route_t = route8.reshape(nt, 1, tc * 2 * TOP_K)
    if rows_per_mod == 1:
        g2_spec = pl.BlockSpec((tc, d), lambda i, ps: (i, 0))
        g2_in = g2
    else:
        tiles_per_mod = rows_per_mod // tc
        g2_spec = pl.BlockSpec((1, 1, d), lambda i, ps: (i // tiles_per_mod, 0, 0))
        g2_in = g2.reshape(g2.shape[0], 1, d)
    grid_spec = pltpu.PrefetchScalarGridSpec(
        num_scalar_prefetch=1,
        grid=(nt,),
        in_specs=[
            pl.BlockSpec((1, 1, tc * 2 * TOP_K), lambda i, ps: (i, 0, 0), memory_space=pltpu.SMEM),
            pl.BlockSpec((tc, LANES), lambda i, ps: (i, 0)),
            pl.BlockSpec((tc, d), lambda i, ps: (i, 0)),
            g2_spec,
            pl.BlockSpec((1, d), lambda i, ps: (0, 0)),
            pl.BlockSpec((1, d), lambda i, ps: (0, 0)),
            pl.BlockSpec(memory_space=pl.ANY),
        ],
        out_specs=pl.BlockSpec((tc, d), lambda i, ps: (i, 0)),
        scratch_shapes=[pltpu.VMEM((TOP_K, tc, ROW_TILES, LANES), F32), pltpu.SemaphoreType.DMA(())],
    )
    kern = _combine_kernel if rows_per_mod == 1 else _combine_kernel_bcast
    return pl.pallas_call(
        kern,
        grid_spec=grid_spec,
        out_shape=jax.ShapeDtypeStruct((n, d), F32),
        compiler_params=_cparams(("arbitrary",)),
        name="moe_combine",
    )(pstart, route_t, gates, x1_2d, g2_in, ln_g, ln_b, yb)


def _combine_kernel_bcast(pstart_ref, route_ref, gate_ref, x1_ref, g2_ref, lg_ref, lb_ref, yb_ref,
                          o_ref, buf_ref, sem):
    _combine_kernel(pstart_ref, route_ref, gate_ref, x1_ref, g2_ref.at[0], lg_ref, lb_ref, yb_ref,
                    o_ref, buf_ref, sem)


def _rope_tables(pos):
    half = QK_DIM // 2
    inv = ROPE_THETA ** (-jnp.arange(half, dtype=F32) / half)
    ang = pos.astype(F32)[:, None] * inv[None, :]
    cos = jnp.cos(ang)
    sin = jnp.sin(ang)
    reps = LANES // QK_DIM
    cos_t = jnp.tile(jnp.concatenate([cos, cos], axis=1), (1, reps))
    sin_t = jnp.tile(jnp.concatenate([-sin, sin], axis=1), (1, reps))
    return cos_t, sin_t


def _pad_rows(a, rows):
    return jnp.concatenate([a, jnp.zeros((rows - a.shape[0],) + a.shape[1:], a.dtype)], axis=0)


def kernel(x_prompt, x_sample, c_prompt, c_sample, cache_k, cache_v, state_conv, page_table, w_ada, b_ada, w_in, conv_w, lambda_q1, lambda_k1, lambda_q2, lambda_k2, subln_g, chunk_ln_g, chunk_ln_b, chunk_ws, chunk_bs, w_out, ln1_g, ln1_b, w_router, b_router, w_gate, b_gate, w_up, b_up, w_down, b_down, ln2_g, ln2_b):
    n_prompt, seq, d = x_prompt.shape
    n_dec, dec_seq, _ = x_sample.shape
    depth = w_in.shape[0]
    n_pool = cache_k.shape[1]
    n_pages = page_table.shape[1]
    past_len = n_pages * PAGE_SIZE
    ns = n_dec * dec_seq
    np_tok = n_prompt * seq
    assert dec_seq >= CONV_K - 1 and dec_seq <= SUBLANES and CHUNK % dec_seq == 0
    assert ns % CHUNK == 0 and seq % CHUNK == 0

    c_all = jnp.concatenate([c_prompt, c_sample], axis=0)
    m_rows = -(-c_all.shape[0] // SUBLANES) * SUBLANES
    mod_all = _ada_call(_pad_rows(c_all, m_rows), w_ada, b_ada)

    cos_p, sin_p = _rope_tables(jnp.arange(seq))
    cos_s, sin_s = _rope_tables(jnp.tile(past_len + jnp.arange(dec_seq), n_dec))

    cache_k4 = jnp.transpose(cache_k, (0, 1, 3, 4, 5, 2)).reshape(depth, n_pool, ATT_HEADS * 2 * QK_DIM, PAGE_SIZE)
    cache_v4 = jnp.transpose(cache_v, (0, 1, 3, 4, 2)).reshape(depth, n_pool, ATT_W, PAGE_SIZE)
    conv0 = jnp.zeros((n_prompt, CONV_K - 1, CONV_W), F32)
    lane = jnp.arange(LANES)
    tok_pos = jnp.arange(ns) % dec_seq

    nrow = 2 * dec_seq * ATT_HEADS
    r_map = jnp.arange(nrow) // (dec_seq * ATT_HEADS)
    r_q = (jnp.arange(nrow) // ATT_HEADS) % dec_seq
    r_h = jnp.arange(nrow) % ATT_HEADS
    col = jnp.arange(ATT_HEADS * 2 * QK_DIM)
    qbd_mask = ((col[None, :] // (2 * QK_DIM)) == r_h[:, None]) & (((col[None, :] // QK_DIM) % 2) == r_map[:, None])

    n_assign = (np_tok + ns) * TOP_K
    n_blocks = -(-n_assign // BM_EXPERT) + N_EXPERTS
    n_rows = n_blocks * BM_EXPERT

    y_p, y_s = x_prompt, x_sample.reshape(ns, d)
    outs = {k: [] for k in ("kp", "vp", "cp", "chp", "ks", "vs", "cs", "chs")}
    for l in range(depth):
        layer = jnp.full((1,), l, jnp.int32)
        lam_init = 0.8 - 0.6 * math.exp(-0.3 * l)
        mod = mod_all[l]
        mp = mod[:n_prompt].reshape(n_prompt, 6, 1, d)
        ms = jnp.repeat(mod[n_prompt:n_prompt + n_dec], dec_seq, axis=0).reshape(ns, 6, d)
        w_in_b = w_in[l].astype(BF16)
        w_out_b = w_out[l].astype(BF16)
        cw = _pad_rows(conv_w[l], SUBLANES)
        lamp = jnp.zeros((SUBLANES, LANES), F32)
        lamp = lamp.at[0, :QK_DIM].set(lambda_q1[l]).at[1, :QK_DIM].set(lambda_k1[l])
        lamp = lamp.at[2, :QK_DIM].set(lambda_q2[l]).at[3, :QK_DIM].set(lambda_k2[l])
        lamp = lamp.at[4, :].set(lam_init)
        lng = chunk_ln_g[l].reshape(1, CHUNK_W)
        lnb = chunk_ln_b[l].reshape(1, CHUNK_W)
        ws = chunk_ws[l]
        bs = chunk_bs[l]
        wcat_p = jnp.concatenate([jnp.tril(ws[g]) for g in range(CHUNK_GROUPS)], axis=1).astype(BF16)
        bsf_p = jnp.repeat(bs.T, CHUNK_GW, axis=1)
        eye = jnp.eye(CHUNK // dec_seq, dtype=F32)
        wcat_s = jnp.concatenate([jnp.kron(eye, jnp.tril(ws[g, :dec_seq, :dec_seq]))
                                  for g in range(CHUNK_GROUPS)], axis=1).astype(BF16)
        bsf_s = jnp.tile(jnp.repeat(bs[:, :dec_seq].T, CHUNK_GW, axis=1), (CHUNK // dec_seq, 1))
        g_sub = subln_g[l]
        wr_pad = jnp.concatenate([w_router[l], jnp.zeros((d, LANES - N_EXPERTS), F32)], axis=1)
        br_pad = jnp.concatenate([b_router[l], jnp.full((LANES - N_EXPERTS,), NEG_BIG, F32)]).reshape(1, LANES)
        l1g, l1b = ln1_g[l].reshape(1, d), ln1_b[l].reshape(1, d)
        l2g, l2b = ln2_g[l].reshape(1, d), ln2_b[l].reshape(1, d)

        (q_p, kf_p, kb_p, vf_p, vb_p, co_p, cho_p, cst_p, chst_p) = _inproj_prompt(
            y_p, mp[:, 1], mp[:, 0], w_in_b, cos_p, sin_p, cw, conv0, lng, lnb, wcat_p, bsf_p)
        att_p = _attn_prompt(q_p, kb_p, vb_p, lamp, jnp.broadcast_to(g_sub[:, None], (V_DIM, vb_p.shape[3])))
        zero_carry = jnp.zeros((SUBLANES, LANES), F32)
        x1_p, h2_p, route_p, gate_p, cnt_p = _outproj(
            y_p, co_p, att_p, cho_p, mp[:, 2], mp[:, 4], mp[:, 3], w_out_b, l1g, l1b, wr_pad, br_pad,
            zero_carry, True)

        st = state_conv[l]
        prev1 = jnp.repeat(st[:, 1], dec_seq, axis=0)
        prev2 = jnp.where((tok_pos == 0)[:, None], jnp.repeat(st[:, 0], dec_seq, axis=0), prev1)
        (q_s, kf_s, kb_s, vf_s, vb_s, co_s, cho_s, z_s, vn_s) = _inproj_sample(
            y_s, ms[:, 1], ms[:, 0], w_in_b, cos_s, sin_s, cw, prev1, prev2, lng, lnb, wcat_s, bsf_s, dec_seq)
        q3 = q_s.reshape(n_dec, dec_seq, -1)
        qbd = jnp.where(qbd_mask[None], q3[:, r_q, :], jnp.zeros((), BF16))
        pad3 = lambda a: jnp.concatenate(
            [a.reshape(n_dec, dec_seq, -1),
             jnp.zeros((n_dec, NEW_PAD - dec_seq, a.shape[-1]), a.dtype)], axis=1)
        att_s = _attn_sample(layer, page_table, cache_k4, cache_v4, qbd, pad3(kf_s), pad3(vf_s), lamp,
                             jnp.tile(g_sub, ATT_HEADS).reshape(1, ATT_W), dec_seq)
        x1_s, h2_s, route_s, gate_s, cnt_all = _outproj(
            y_s, co_s, att_s.reshape(ns, ATT_W).astype(BF16), cho_s, ms[:, 2], ms[:, 4], ms[:, 3],
            w_out_b, l1g, l1b, wr_pad, br_pad, cnt_p, False)

        counts = cnt_all[0, :N_EXPERTS].astype(jnp.int32)
        pcounts = ((counts + BM_EXPERT - 1) // BM_EXPERT) * BM_EXPERT
        pends = jnp.cumsum(pcounts)
        pstart = (pends - pcounts).astype(jnp.int32)
        blk_row = jnp.arange(n_blocks, dtype=jnp.int32) * BM_EXPERT
        block_e = jnp.minimum(jnp.sum(blk_row[:, None] >= pends[None, :], axis=1), N_EXPERTS - 1).astype(jnp.int32)
        n_used = (pends[-1] // BM_EXPERT).astype(jnp.int32).reshape(1)
        route8_p = route_p.reshape(np_tok, LANES)[:, :2 * TOP_K]
        route8_s = route_s[:, :2 * TOP_K]
        xs = jnp.zeros((n_rows, ROW_TILES, LANES), F32)
        xs = _dispatch(pstart, route8_p, h2_p.reshape(np_tok, ROW_TILES, LANES), xs)
        xs = _dispatch(pstart, route8_s, h2_s, xs)
        yb = _experts(layer, block_e, n_used, xs, w_gate, b_gate, w_up, b_up, w_down, b_down)
        y_p = _combine(pstart, route8_p, gate_p.reshape(np_tok, LANES), x1_p.reshape(np_tok, d),
                       mod[:n_prompt, 5 * d:], l2g, l2b, yb, seq).reshape(n_prompt, seq, d)
        y_s = _combine(pstart, route8_s, gate_s, x1_s, ms[:, 5], l2g, l2b, yb, 1)

        outs["kp"].append(jnp.transpose(kf_p.reshape(n_prompt, ATT_HEADS, 2, QK_DIM, seq), (0, 4, 1, 2, 3)))
        outs["vp"].append(jnp.transpose(vf_p.reshape(n_prompt, ATT_HEADS, V_DIM, seq), (0, 3, 1, 2)))
        outs["cp"].append(cst_p)
        outs["chp"].append(chst_p)
        outs["ks"].append(kf_s.reshape(n_dec, dec_seq, ATT_HEADS, 2, QK_DIM))
        outs["vs"].append(vf_s.reshape(n_dec, dec_seq, ATT_HEADS, V_DIM))
        outs["cs"].append(z_s.reshape(n_dec, dec_seq, CONV_W)[:, dec_seq - (CONV_K - 1):])
        outs["chs"].append(vn_s.reshape(n_dec, dec_seq, CHUNK_W))

    st_ = lambda k: jnp.stack(outs[k])
    return (y_p, y_s.reshape(n_dec, dec_seq, d), st_("kp"), st_("vp"), st_("cp"), st_("chp"),
            st_("ks"), st_("vs"), st_("cs"), st_("chs"))
```

```python
import functools
import math

import jax
import jax.numpy as jnp
from jax import lax
from jax.experimental import pallas as pl
from jax.experimental.pallas import tpu as pltpu

F32 = jnp.float32
BF16 = jnp.bfloat16

D_MODEL = 1024
CONV_W = 256
CONV_K = 3
ATT_HEADS = 8
QK_DIM = 32
V_DIM = 64
ATT_W = ATT_HEADS * V_DIM
CHUNK_GROUPS = 4
CHUNK_GW = 64
CHUNK_W = CHUNK_GROUPS * CHUNK_GW
CHUNK = 128
O_CH = 0
O_Q = 3 * CONV_W
O_K = O_Q + ATT_HEADS * 2 * QK_DIM
O_V = O_K + ATT_HEADS * 2 * QK_DIM
O_U = O_V + ATT_W
O_SV = O_U + CHUNK_W
IN_W = O_SV + CHUNK_W
N_EXPERTS = 32
TOP_K = 4
D_FF = 1024
SWIGLU_ALPHA = 1.702
SWIGLU_LIMIT = 7.0
PAGE_SIZE = 128
ROPE_THETA = 10000.0
LN_EPS = 1e-5
DEPTH = 4
DN_ALPHA = (2 * DEPTH) ** 0.25
Q_SCALE = QK_DIM ** -0.5 * math.log2(math.e)

LANES = 128
SUBLANES = 8
ROW_TILES = D_MODEL // LANES
VMEM_BYTES_V7X = 64 * 1024 * 1024

TM_PROJ = 512
TQ_ATT = 512
TK_ATT = 512
PAGES_PER_STEP = 8
BM_EXPERT = 256
TD_ROWS = 256
NEW_PAD = PAGE_SIZE
NEG_BIG = -1e30
VMEM_LIMIT = 56 * 1024 * 1024


def _cparams(sem):
    return pltpu.CompilerParams(dimension_semantics=sem, vmem_limit_bytes=VMEM_LIMIT)


def _split_bf16(a):
    hi = a.astype(BF16)
    lo = (a - hi.astype(F32)).astype(BF16)
    return hi, lo


def _dot(a, b):
    return jnp.dot(a, b, preferred_element_type=F32)


def _dot_split_lhs(a, b_bf16):
    hi, lo = _split_bf16(a)
    return _dot(hi, b_bf16) + _dot(lo, b_bf16)


def _dot3(a, b):
    ah, al = _split_bf16(a)
    bh, bl = _split_bf16(b)
    return _dot(ah, bh) + (_dot(al, bh) + _dot(ah, bl))


def _store_rows_as_tiles(ref, val):
    rows = val.shape[0]
    for j in range(ROW_TILES):
        ref[pl.ds(j, rows, stride=ROW_TILES), :] = val[:, j * LANES:(j + 1) * LANES]


def _load_rows_from_tiles(ref):
    rows = ref.shape[0] // ROW_TILES
    return jnp.concatenate([ref[pl.ds(j, rows, stride=ROW_TILES), :] for j in range(ROW_TILES)], axis=1)


def _layer_norm_rows(v, g, b):
    mu = jnp.mean(v, axis=-1, keepdims=True)
    d = v - mu
    var = jnp.mean(d * d, axis=-1, keepdims=True)
    return d * lax.rsqrt(var + LN_EPS) * g + b


def _ada_kernel(c_ref, w_ref, b_ref, o_ref):
    o_ref[0] = _dot3(c_ref[...], w_ref[0]) + b_ref[0]


def _ada_call(c_all, w_ada, b_ada):
    depth, d, n6 = w_ada.shape
    m = c_all.shape[0]
    tn = 1024
    return pl.pallas_call(
        _ada_kernel,
        grid=(depth, n6 // tn),
        in_specs=[
            pl.BlockSpec((m, d), lambda l, j: (0, 0)),
            pl.BlockSpec((1, d, tn), lambda l, j: (l, 0, j)),
            pl.BlockSpec((1, 1, tn), lambda l, j: (l, 0, j)),
        ],
        out_specs=pl.BlockSpec((1, m, tn), lambda l, j: (l, 0, j)),
        out_shape=jax.ShapeDtypeStruct((depth, m, n6), F32),
        compiler_params=_cparams(("arbitrary", "arbitrary")),
        name="ada",
    )(c_all, w_ada, b_ada.reshape(depth, 1, n6))


def _rope_chunks(p, cos_t, sin_t, scale):
    lane = lax.broadcasted_iota(jnp.int32, (1, LANES), 1)
    first_half = (lane % QK_DIM) < (QK_DIM // 2)
    outs = []
    for c in range(p.shape[1] // LANES):
        xc = p[:, c * LANES:(c + 1) * LANES]
        up = pltpu.roll(xc, LANES - QK_DIM // 2, 1)
        dn = pltpu.roll(xc, QK_DIM // 2, 1)
        r = xc * cos_t + jnp.where(first_half, up, dn) * sin_t
        if scale is not None:
            r = r * scale
        outs.append(r)
    return jnp.concatenate(outs, axis=1)


def _group_norm(sv, g, b):
    ri = lax.broadcasted_iota(jnp.int32, (CHUNK_W, CHUNK_W), 0) // CHUNK_GW
    ci = lax.broadcasted_iota(jnp.int32, (CHUNK_W, CHUNK_W), 1) // CHUNK_GW
    avg = jnp.where(ri == ci, 1.0 / CHUNK_GW, 0.0).astype(BF16)
    mu = _dot_split_lhs(sv, avg)
    d = sv - mu
    var = _dot_split_lhs(d * d, avg)
    return d * lax.rsqrt(var + LN_EPS) * g + b


def _chunk_mix(vn, wcat_ref, bsf_ref):
    gid = lax.broadcasted_iota(jnp.int32, (1, CHUNK_W), 1) // CHUNK_GW
    vb = vn.astype(BF16)
    zero = jnp.zeros_like(vb[:CHUNK])
    outs = []
    for c in range(vn.shape[0] // CHUNK):
        vc = vb[c * CHUNK:(c + 1) * CHUNK]
        stack = jnp.concatenate([jnp.where(gid == g, vc, zero) for g in range(CHUNK_GROUPS)], axis=0)
        outs.append(_dot(wcat_ref[...], stack) + bsf_ref[...])
    return jnp.concatenate(outs, axis=0)


def _inproj_kernel(*refs, seq_mode, dec_seq):
    if seq_mode:
        (x_ref, sc_ref, sh_ref, w_ref, cos_ref, sin_ref, cw_ref, cprev_ref, lng_ref, lnb_ref,
         wcat_ref, bsf_ref,
         q_ref, kf_ref, kb_ref, vf_ref, vb_ref, co_ref, cho_ref, cst_ref, chst_ref, zc_ref) = refs
        x = x_ref[0]
        sc = sc_ref[0]
        sh = sh_ref[0]
    else:
        (x_ref, sc_ref, sh_ref, w_ref, cos_ref, sin_ref, cw_ref, p1_ref, p2_ref, lng_ref, lnb_ref,
         wcat_ref, bsf_ref,
         q_ref, kf_ref, kb_ref, vf_ref, vb_ref, co_ref, cho_ref, cst_ref, chst_ref) = refs
        x = x_ref[...]
        sc = sc_ref[...]
        sh = sh_ref[...]
    tm = x.shape[0]
    h = (x * (1.0 + sc) + sh).astype(BF16)

    pc = _dot(h, w_ref[:, O_CH:O_Q])
    conv_in, gate_b, gate_c = pc[:, :CONV_W], pc[:, CONV_W:2 * CONV_W], pc[:, 2 * CONV_W:]
    z = gate_c * conv_in
    row = lax.broadcasted_iota(jnp.int32, (tm, 1), 0)
    z1 = pltpu.roll(z, 1, 0)
    z2 = pltpu.roll(z, 2, 0)
    if seq_mode:
        @pl.when(pl.program_id(1) == 0)
        def _():
            zc_ref[0:2, :] = cprev_ref[0]
        c0 = zc_ref[0:1, :]
        c1 = zc_ref[1:2, :]
        z1 = jnp.where(row == 0, c1, z1)
        z2 = jnp.where(row == 0, c0, jnp.where(row == 1, c1, z2))
        zc_ref[0:2, :] = z[tm - 2:tm, :]
        cst_ref[0] = z[tm - 2:tm, :]
    else:
        pos_in_seq = row % dec_seq
        z1 = jnp.where(pos_in_seq >= 1, z1, p1_ref[...])
        z2 = jnp.where(pos_in_seq >= 2, z2, p2_ref[...])
        cst_ref[...] = z
    y = cw_ref[0:1, :] * z2 + cw_ref[1:2, :] * z1 + cw_ref[2:3, :] * z
    conv_out = (gate_b * y).astype(BF16)

    pqk = _dot(h, w_ref[:, O_Q:O_V])
    cos_t = cos_ref[...]
    sin_t = sin_ref[...]
    nq = O_K - O_Q
    qr = _rope_chunks(pqk[:, :nq], cos_t, sin_t, Q_SCALE)
    kr = _rope_chunks(pqk[:, nq:], cos_t, sin_t, None)
    pv = _dot(h, w_ref[:, O_V:O_U])

    pu = _dot(h, w_ref[:, O_U:IN_W])
    u, sv = pu[:, :CHUNK_W], pu[:, CHUNK_W:]
    vn = _group_norm(sv, lng_ref[...], lnb_ref[...])
    chunk_out = (u * _chunk_mix(vn, wcat_ref, bsf_ref)).astype(BF16)

    if seq_mode:
        q_ref[0] = qr.T.astype(BF16)
        kf_ref[0] = kr.T
        kb_ref[0] = kr.astype(BF16)
        vt = pv.T
        vf_ref[0] = vt
        vtb = vt.astype(BF16)
        for c in range(tm // TK_ATT):
            vb_ref[0, c] = vtb[:, c * TK_ATT:(c + 1) * TK_ATT]
        co_ref[0] = conv_out
        cho_ref[0] = chunk_out
        chst_ref[0] = vn[tm - CHUNK:tm, :]
    else:
        q_ref[...] = qr.astype(BF16)
        kf_ref[...] = kr
        kb_ref[...] = kr.astype(BF16)
        vf_ref[...] = pv
        vb_ref[...] = pv.astype(BF16)
        co_ref[...] = conv_out
        cho_ref[...] = chunk_out
        chst_ref[...] = vn


def _inproj_prompt(x, sc1, sh1, w_in_b, cos_t, sin_t, conv_w, conv_prev, ln_g, ln_b, wcat, bsf):
    b_, t_, d = x.shape
    tm = min(TM_PROJ, t_)
    nt = t_ // tm
    const2 = lambda b, i: (0, 0)
    tok3 = lambda b, i: (b, i, 0)
    tokT = lambda b, i: (b, 0, i)
    bat3 = lambda b, i: (b, 0, 0)
    qk_w = ATT_HEADS * 2 * QK_DIM
    outs = pl.pallas_call(
        functools.partial(_inproj_kernel, seq_mode=True, dec_seq=0),
        grid=(b_, nt),
        in_specs=[
            pl.BlockSpec((1, tm, d), tok3),
            pl.BlockSpec((1, 1, d), bat3),
            pl.BlockSpec((1, 1, d), bat3),
            pl.BlockSpec((d, IN_W), const2),
            pl.BlockSpec((tm, LANES), lambda b, i: (i, 0)),
            pl.BlockSpec((tm, LANES), lambda b, i: (i, 0)),
            pl.BlockSpec((SUBLANES, CONV_W), const2),
            pl.BlockSpec((1, CONV_K - 1, CONV_W), bat3),
            pl.BlockSpec((1, CHUNK_W), const2),
            pl.BlockSpec((1, CHUNK_W), const2),
            pl.BlockSpec((CHUNK, CHUNK_GROUPS * CHUNK), const2),
            pl.BlockSpec((CHUNK, CHUNK_W), const2),
        ],
        out_specs=[
            pl.BlockSpec((1, qk_w, tm), tokT),
            pl.BlockSpec((1, qk_w, tm), tokT),
            pl.BlockSpec((1, tm, qk_w), tok3),
            pl.BlockSpec((1, ATT_W, tm), tokT),
            pl.BlockSpec((1, tm // TK_ATT, ATT_W, TK_ATT), lambda b, i: (b, i, 0, 0)),
            pl.BlockSpec((1, tm, CONV_W), tok3),
            pl.BlockSpec((1, tm, CHUNK_W), tok3),
            pl.BlockSpec((1, CONV_K - 1, CONV_W), bat3),
            pl.BlockSpec((1, CHUNK, CHUNK_W), bat3),
        ],
        out_shape=[
            jax.ShapeDtypeStruct((b_, qk_w, t_), BF16),
            jax.ShapeDtypeStruct((b_, qk_w, t_), F32),
            jax.ShapeDtypeStruct((b_, t_, qk_w), BF16),
            jax.ShapeDtypeStruct((b_, ATT_W, t_), F32),
            jax.ShapeDtypeStruct((b_, t_ // TK_ATT, ATT_W, TK_ATT), BF16),
            jax.ShapeDtypeStruct((b_, t_, CONV_W), BF16),
            jax.ShapeDtypeStruct((b_, t_, CHUNK_W), BF16),
            jax.ShapeDtypeStruct((b_, CONV_K - 1, CONV_W), F32),
            jax.ShapeDtypeStruct((b_, CHUNK, CHUNK_W), F32),
        ],
        scratch_shapes=[pltpu.VMEM((SUBLANES, CONV_W), F32)],
        compiler_params=_cparams(("arbitrary", "arbitrary")),
        name="inproj_prompt",
    )(x, sc1, sh1, w_in_b, cos_t, sin_t, conv_w, conv_prev, ln_g, ln_b, wcat, bsf)
    return outs


def _inproj_sample(x2d, sc1, sh1, w_in_b, cos_t, sin_t, conv_w, prev1, prev2, ln_g, ln_b, wcat, bsf,
                   dec_seq):
    n, d = x2d.shape
    qk_w = ATT_HEADS * 2 * QK_DIM
    full = lambda shape: pl.BlockSpec(shape, lambda i: tuple(0 for _ in shape))
    outs = pl.pallas_call(
        functools.partial(_inproj_kernel, seq_mode=False, dec_seq=dec_seq),
        grid=(1,),
        in_specs=[
            full((n, d)), full((n, d)), full((n, d)), full((d, IN_W)),
            full((n, LANES)), full((n, LANES)), full((SUBLANES, CONV_W)),
            full((n, CONV_W)), full((n, CONV_W)), full((1, CHUNK_W)), full((1, CHUNK_W)),
            full((CHUNK, CHUNK_GROUPS * CHUNK)), full((CHUNK, CHUNK_W)),
        ],
        out_specs=[
            full((n, qk_w)), full((n, qk_w)), full((n, qk_w)), full((n, ATT_W)), full((n, ATT_W)),
            full((n, CONV_W)), full((n, CHUNK_W)), full((n, CONV_W)), full((n, CHUNK_W)),
        ],
        out_shape=[
            jax.ShapeDtypeStruct((n, qk_w), BF16),
            jax.ShapeDtypeStruct((n, qk_w), F32),
            jax.ShapeDtypeStruct((n, qk_w), BF16),
            jax.ShapeDtypeStruct((n, ATT_W), F32),
            jax.ShapeDtypeStruct((n, ATT_W), BF16),
            jax.ShapeDtypeStruct((n, CONV_W), BF16),
            jax.ShapeDtypeStruct((n, CHUNK_W), BF16),
            jax.ShapeDtypeStruct((n, CONV_W), F32),
            jax.ShapeDtypeStruct((n, CHUNK_W), F32),
        ],
        compiler_params=_cparams(("arbitrary",)),
        name="inproj_sample",
    )(x2d, sc1, sh1, w_in_b, cos_t, sin_t, conv_w, prev1, prev2, ln_g, ln_b, wcat, bsf)
    return outs


def _lambda_from(lam_ref):
    lp = lam_ref[...]
    s1 = jnp.sum(lp[0:1, :] * lp[1:2, :], axis=1, keepdims=True)
    s2 = jnp.sum(lp[2:3, :] * lp[3:4, :], axis=1, keepdims=True)
    lam_init = lp[4:5, 0:1]
    return jnp.exp(s1) - jnp.exp(s2) + lam_init, lam_init


L_ROWS = 16


def _attn_prompt_kernel(qt_ref, k_ref, vt_ref, lam_ref, g_ref, o_ref, qm_ref, m_ref, acc_ref):
    tq = qt_ref.shape[2]
    tk = vt_ref.shape[3]
    qi = pl.program_id(2)
    n_hm = 2 * 2

    qt = qt_ref[0]
    row_hm = lax.broadcasted_iota(jnp.int32, (LANES, 1), 0) // QK_DIM
    zero = jnp.zeros_like(qt)
    for hm in range(n_hm):
        qm_ref[hm] = jnp.where(row_hm == hm, qt, zero)
    m_ref[...] = jnp.full(m_ref.shape, NEG_BIG, F32)
    acc_ref[...] = jnp.zeros(acc_ref.shape, F32)

    def step(ki, diag_offset):
        masked = diag_offset is not None
        start = pl.multiple_of(ki * tk, tk)
        kt = k_ref[0, pl.ds(start, tk), :]
        vt = vt_ref[0, ki]
        ones = jnp.ones((L_ROWS, tk), BF16)
        if masked:
            key = lax.broadcasted_iota(jnp.int32, (tk, tq), 0) + diag_offset
            qry = lax.broadcasted_iota(jnp.int32, (tk, tq), 1)
            valid = key <= qry
        m_old = [m_ref[hm] for hm in range(n_hm)]
        acc_old = [acc_ref[hm] for hm in range(n_hm)]
        lhs = [jnp.concatenate([vt[h * V_DIM:(h + 1) * V_DIM, :], ones], axis=0) for h in range(2)]
        s = [_dot(kt, qm_ref[hm]) for hm in range(n_hm)]
        if masked:
            s = [jnp.where(valid, s_, NEG_BIG) for s_ in s]
        m_new = [jnp.maximum(m_old[hm], jnp.max(s[hm], axis=0, keepdims=True)) for hm in range(n_hm)]
        p = [jnp.exp2(s[hm] - m_new[hm]).astype(BF16) for hm in range(n_hm)]
        pv = [_dot(lhs[hm // 2], p[hm]) for hm in range(n_hm)]
        for hm in range(n_hm):
            acc_ref[hm] = jnp.exp2(m_old[hm] - m_new[hm]) * acc_old[hm] + pv[hm]
            m_ref[hm] = m_new[hm]

    def body(ki, carry):
        step(ki, None)
        return carry

    ratio = tq // tk
    lax.fori_loop(0, qi * ratio, body, 0)
    for j in range(ratio):
        step(qi * ratio + j, j * tk)

    lam, lam_init = _lambda_from(lam_ref)
    halves = []
    for h in range(2):
        a0 = acc_ref[2 * h]
        a1 = acc_ref[2 * h + 1]
        o0 = a0[:V_DIM] * (1.0 / a0[V_DIM:V_DIM + 1])
        o1 = a1[:V_DIM] * (1.0 / a1[V_DIM:V_DIM + 1])
        d = o0 - lam * o1
        ms = jnp.sum(d * d, axis=0, keepdims=True) * (1.0 / V_DIM)
        halves.append(d * lax.rsqrt(ms + LN_EPS) * g_ref[...] * (1.0 - lam_init))
    o_ref[0] = jnp.concatenate(halves, axis=0).T.astype(o_ref.dtype)


def _attn_prompt(qt, k, vt4, lamp, g_col):
    b_, t_, _ = k.shape
    tk = vt4.shape[3]
    tq = g_col.shape[1]
    nq = t_ // tq
    npair = ATT_HEADS // 2
    return pl.pallas_call(
        _attn_prompt_kernel,
        grid=(b_, npair, nq),
        in_specs=[
            pl.BlockSpec((1, LANES, tq), lambda b, h, i: (b, h, i)),
            pl.BlockSpec((1, t_, LANES), lambda b, h, i: (b, 0, h)),
            pl.BlockSpec((1, t_ // tk, LANES, tk), lambda b, h, i: (b, 0, h, 0)),
            pl.BlockSpec((SUBLANES, LANES), lambda b, h, i: (0, 0)),
            pl.BlockSpec((V_DIM, tq), lambda b, h, i: (0, 0)),
        ],
        out_specs=pl.BlockSpec((1, tq, LANES), lambda b, h, i: (b, i, h)),
        out_shape=jax.ShapeDtypeStruct((b_, t_, ATT_W), BF16),
        scratch_shapes=[
            pltpu.VMEM((4, LANES, tq), BF16),
            pltpu.VMEM((4, 1, tq), F32),
            pltpu.VMEM((4, V_DIM + L_ROWS, tq), F32),
        ],
        compiler_params=_cparams(("arbitrary", "arbitrary", "arbitrary")),
        name="attn_prompt",
    )(qt, k, vt4, lamp, g_col)


def _attn_sample_kernel(l_ref, pt_ref, *refs, pps, dec_seq):
    k_refs = refs[:pps]
    v_refs = refs[pps:2 * pps]
    qbd_ref, kn_ref, vn_ref, lam_ref, g_ref, o_ref, m_ref, l_acc_ref, acc_ref = refs[2 * pps:]
    j = pl.program_id(1)
    nrow = qbd_ref.shape[1]
    half = nrow // 2

    @pl.when(j == 0)
    def _():
        m_ref[...] = jnp.full(m_ref.shape, NEG_BIG, F32)
        l_acc_ref[...] = jnp.zeros(l_acc_ref.shape, F32)
        acc_ref[...] = jnp.zeros(acc_ref.shape, F32)

    qbd = qbd_ref[0]

    nt = (((1,), (1,)), ((), ()))

    def online(s, vals, vals_transposed):
        m_old = m_ref[...]
        m_new = jnp.maximum(m_old, jnp.max(s, axis=1, keepdims=True))
        alpha = jnp.exp2(m_old - m_new)
        p = jnp.exp2(s - m_new)
        m_ref[...] = m_new
        l_acc_ref[...] = alpha * l_acc_ref[...] + jnp.sum(p, axis=1, keepdims=True)
        pb = p.astype(BF16)
        pv = None
        for r, val in enumerate(vals):
            pr = pb[:, r * PAGE_SIZE:(r + 1) * PAGE_SIZE]
            if vals_transposed:
                t = lax.dot_general(pr, val, nt, preferred_element_type=F32)
            else:
                t = _dot(pr, val)
            pv = t if pv is None else pv + t
        acc_ref[...] = alpha * acc_ref[...] + pv

    s_parts = [_dot(qbd, k_refs[r][0, 0].astype(BF16)) for r in range(pps)]
    online(jnp.concatenate(s_parts, axis=1), [v_refs[r][0, 0].astype(BF16) for r in range(pps)], True)

    @pl.when(j == pl.num_programs(1) - 1)
    def _():
        kn = kn_ref[0].astype(BF16)
        s = lax.dot_general(qbd, kn, nt, preferred_element_type=F32)
        r_ = lax.broadcasted_iota(jnp.int32, s.shape, 0)
        c_ = lax.broadcasted_iota(jnp.int32, s.shape, 1)
        qpos = (r_ % half) // ATT_HEADS
        s = jnp.where(c_ <= qpos, s, NEG_BIG)
        online(s, [vn_ref[0].astype(BF16)], False)

        lam, lam_init = _lambda_from(lam_ref)
        o = acc_ref[...] / l_acc_ref[...]
        d = o[:half] - lam * o[half:]
        rr = lax.broadcasted_iota(jnp.int32, d.shape, 0) % ATT_HEADS
        cc = lax.broadcasted_iota(jnp.int32, d.shape, 1) // V_DIM
        d = jnp.where(rr == cc, d, 0.0)
        ms = jnp.sum(d * d, axis=1, keepdims=True) * (1.0 / V_DIM)
        dn = d * lax.rsqrt(ms + LN_EPS) * g_ref[...] * (1.0 - lam_init)
        o_ref[0] = jnp.sum(dn.reshape(dec_seq, ATT_HEADS, ATT_W), axis=1)


def _attn_sample(layer, page_table, cache_k4, cache_v4, qbd, kn_pad, vn_pad, lamp, g8, dec_seq):
    n_dec, n_pages = page_table.shape
    pps = PAGES_PER_STEP
    while n_pages % pps:
        pps //= 2
    nrow = qbd.shape[1]
    qkw = qbd.shape[2]

    def page_map(r):
        return lambda b, j, l_ref, pt_ref: (l_ref[0], pt_ref[b * n_pages + j * pps + r], 0, 0)

    seq3 = lambda b, j, l_ref, pt_ref: (b, 0, 0)
    const2 = lambda b, j, l_ref, pt_ref: (0, 0)
    in_specs = ([pl.BlockSpec((1, 1, qkw, PAGE_SIZE), page_map(r)) for r in range(pps)]
                + [pl.BlockSpec((1, 1, ATT_W, PAGE_SIZE), page_map(r)) for r in range(pps)]
                + [pl.BlockSpec((1, nrow, qkw), seq3),
                   pl.BlockSpec((1, NEW_PAD, qkw), seq3),
                   pl.BlockSpec((1, NEW_PAD, ATT_W), seq3),
                   pl.BlockSpec((SUBLANES, LANES), const2),
                   pl.BlockSpec((1, ATT_W), const2)])
    grid_spec = pltpu.PrefetchScalarGridSpec(
        num_scalar_prefetch=2,
        grid=(n_dec, n_pages // pps),
        in_specs=in_specs,
        out_specs=pl.BlockSpec((1, dec_seq, ATT_W), seq3),
        scratch_shapes=[
            pltpu.VMEM((nrow, 1), F32),
            pltpu.VMEM((nrow, 1), F32),
            pltpu.VMEM((nrow, ATT_W), F32),
        ],
    )
    return pl.pallas_call(
        functools.partial(_attn_sample_kernel, pps=pps, dec_seq=dec_seq),
        grid_spec=grid_spec,
        out_shape=jax.ShapeDtypeStruct((n_dec, dec_seq, ATT_W), F32),
        compiler_params=_cparams(("arbitrary", "arbitrary")),
        name="attn_sample",
    )(layer, page_table.reshape(-1), *([cache_k4] * pps), *([cache_v4] * pps),
      qbd, kn_pad, vn_pad, lamp, g8)


def _outproj_kernel(x_ref, co_ref, att_ref, cho_ref, g1_ref, sc2_ref, sh2_ref, wo_ref, lg_ref, lb_ref,
                    wr_ref, br_ref, cin_ref,
                    x1_ref, h2_ref, route_ref, gate_ref, cnt_ref, carry_ref, *, batched):
    first = (pl.program_id(0) == 0) & (pl.program_id(1) == 0)

    @pl.when(first)
    def _():
        carry_ref[...] = cin_ref[...]

    if batched:
        x, co, att, cho = x_ref[0], co_ref[0], att_ref[0], cho_ref[0]
        g1, sc2, sh2 = g1_ref[0], sc2_ref[0], sh2_ref[0]
    else:
        x, co, att, cho = x_ref[...], co_ref[...], att_ref[...], cho_ref[...]
        g1, sc2, sh2 = g1_ref[...], sc2_ref[...], sh2_ref[...]
    tm = x.shape[0]
    mix = jnp.concatenate([co, att, cho], axis=1)
    y = _dot(mix, wo_ref[...])
    x1 = _layer_norm_rows(DN_ALPHA * x + (1.0 + g1) * y, lg_ref[...], lb_ref[...])
    h2 = x1 * (1.0 + sc2) + sh2

    logits = _dot3(h2, wr_ref[...]) + br_ref[...]
    lane = lax.broadcasted_iota(jnp.int32, (1, LANES), 1)
    lane_f = lane.astype(F32)
    work = logits
    vals, hots, idxs = [], [], []
    for _ in range(TOP_K):
        mx = jnp.max(work, axis=1, keepdims=True)
        idx = jnp.min(jnp.where(work == mx, lane_f, float(LANES)), axis=1, keepdims=True)
        hot = lane_f == idx
        work = jnp.where(hot, -3e38, work)
        vals.append(mx)
        hots.append(hot)
        idxs.append(idx)
    exps = [jnp.exp(v - vals[0]) for v in vals]
    inv = 1.0 / (exps[0] + exps[1] + exps[2] + exps[3])

    any_hot = hots[0] | hots[1] | hots[2] | hots[3]
    a_b = jnp.where(any_hot, 1.0, 0.0).astype(BF16)
    ri = lax.broadcasted_iota(jnp.int32, (tm, tm), 0)
    ci = lax.broadcasted_iota(jnp.int32, (tm, tm), 1)
    stril = jnp.where(ci < ri, 1.0, 0.0).astype(BF16)
    before = _dot(stril, a_b) + carry_ref[0:1, :]
    new_carry = carry_ref[0:1, :] + jnp.sum(a_b.astype(F32), axis=0, keepdims=True)
    carry_ref[0:1, :] = new_carry

    route = jnp.zeros((tm, LANES), F32)
    gates = jnp.zeros((tm, LANES), F32)
    for k in range(TOP_K):
        rank_k = jnp.sum(jnp.where(hots[k], before, 0.0), axis=1, keepdims=True)
        route = jnp.where(lane == k, idxs[k], route)
        route = jnp.where(lane == TOP_K + k, rank_k, route)
        gates = jnp.where(lane == k, exps[k] * inv, gates)

    if batched:
        x1_ref[0] = x1
        _store_rows_as_tiles(h2_ref.at[0], h2)
        route_ref[0] = route.astype(jnp.int32)
        gate_ref[0] = gates
    else:
        x1_ref[...] = x1
        _store_rows_as_tiles(h2_ref, h2)
        route_ref[...] = route.astype(jnp.int32)
        gate_ref[...] = gates
    cnt_ref[...] = jnp.broadcast_to(new_carry, cnt_ref.shape)


def _outproj(x, co, att, cho, g1, sc2, sh2, w_out_b, ln_g, ln_b, wr_pad, br_pad, carry_in, batched):
    if batched:
        b_, t_, d = x.shape
        tm = min(TM_PROJ, t_)
        grid = (b_, t_ // tm)
        tok = lambda w: pl.BlockSpec((1, tm, w), lambda b, i: (b, i, 0))
        mod = pl.BlockSpec((1, 1, d), lambda b, i: (b, 0, 0))
        oshape = lambda w, dt: jax.ShapeDtypeStruct((b_, t_, w), dt)
        h2_spec = pl.BlockSpec((1, tm * ROW_TILES, LANES), lambda b, i: (b, i, 0))
        h2_shape = jax.ShapeDtypeStruct((b_, t_ * ROW_TILES, LANES), F32)
    else:
        n, d = x.shape
        tm = n
        grid = (1, 1)
        tok = lambda w: pl.BlockSpec((n, w), lambda b, i: (0, 0))
        mod = tok(d)
        oshape = lambda w, dt: jax.ShapeDtypeStruct((n, w), dt)
        h2_spec = pl.BlockSpec((n * ROW_TILES, LANES), lambda b, i: (0, 0))
        h2_shape = jax.ShapeDtypeStruct((n * ROW_TILES, LANES), F32)
    const2 = lambda b, i: (0, 0)
    return pl.pallas_call(
        functools.partial(_outproj_kernel, batched=batched),
        grid=grid,
        in_specs=[
            tok(d), tok(CONV_W), tok(ATT_W), tok(CHUNK_W), mod, mod, mod,
            pl.BlockSpec((d, d), const2),
            pl.BlockSpec((1, d), const2),
            pl.BlockSpec((1, d), const2),
            pl.BlockSpec((d, LANES), const2),
            pl.BlockSpec((1, LANES), const2),
            pl.BlockSpec((SUBLANES, LANES), const2),
        ],
        out_specs=[tok(d), h2_spec, tok(LANES), tok(LANES), pl.BlockSpec((SUBLANES, LANES), const2)],
        out_shape=[oshape(d, F32), h2_shape, oshape(LANES, jnp.int32), oshape(LANES, F32),
                   jax.ShapeDtypeStruct((SUBLANES, LANES), F32)],
        scratch_shapes=[pltpu.VMEM((SUBLANES, LANES), F32)],
        compiler_params=_cparams(("arbitrary", "arbitrary")),
        name="outproj_prompt" if batched else "outproj_sample",
    )(x, co, att, cho, g1, sc2, sh2, w_out_b, ln_g, ln_b, wr_pad, br_pad, carry_in)


def _dispatch_kernel(dest_ref, h_ref, xs_in_ref, xs_ref, sem):
    del xs_in_ref
    td = h_ref.shape[0] // ROW_TILES

    def row_copy(r, dst):
        return pltpu.make_async_copy(h_ref.at[pl.ds(pl.multiple_of(r * ROW_TILES, ROW_TILES), ROW_TILES)],
                                     xs_ref.at[pl.ds(pl.multiple_of(dst * ROW_TILES, ROW_TILES), ROW_TILES)], sem)

    def issue(r, carry):
        for k in range(TOP_K):
            row_copy(r, dest_ref[0, 0, r * TOP_K + k]).start()
        return carry

    lax.fori_loop(0, td, issue, 0, unroll=4)

    def drain(i, carry):
        row_copy(0, 0).wait()
        return carry

    lax.fori_loop(0, td * TOP_K, drain, 0, unroll=8)


def _dispatch(dest, h3d, xs):
    n = h3d.shape[0] // ROW_TILES
    td = min(TD_ROWS, n)
    nt = n // td
    return pl.pallas_call(
        _dispatch_kernel,
        grid=(nt,),
        in_specs=[
            pl.BlockSpec((1, 1, td * TOP_K), lambda i: (i, 0, 0), memory_space=pltpu.SMEM),
            pl.BlockSpec((td * ROW_TILES, LANES), lambda i: (i, 0)),
            pl.BlockSpec(memory_space=pl.ANY),
        ],
        out_specs=pl.BlockSpec(memory_space=pl.ANY),
        scratch_shapes=[pltpu.SemaphoreType.DMA(())],
        out_shape=jax.ShapeDtypeStruct(xs.shape, xs.dtype),
        input_output_aliases={2: 0},
        compiler_params=_cparams(("arbitrary",)),
        name="moe_dispatch",
    )(dest.reshape(nt, 1, td * TOP_K), h3d, xs)


def _expert_kernel(l_ref, be_ref, nu_ref, x_ref, wg_ref, bg_ref, wu_ref, bu_ref, wd_ref, bd_ref,
                   y_ref, wgb_ref, wub_ref, wdb_ref):
    b = pl.program_id(0)
    changed = (b == 0) | (be_ref[b] != be_ref[jnp.maximum(b - 1, 0)])

    @pl.when(changed)
    def _():
        wgb_ref[...] = wg_ref[0, 0].astype(BF16)
        wub_ref[...] = wu_ref[0, 0].astype(BF16)
        wdb_ref[...] = wd_ref[0, 0].astype(BF16)

    @pl.when(b < nu_ref[0])
    def _():
        x = _load_rows_from_tiles(x_ref).astype(BF16)
        g = jnp.minimum(_dot(x, wgb_ref[...]) + bg_ref[0, 0], SWIGLU_LIMIT)
        u = jnp.clip(_dot(x, wub_ref[...]) + bu_ref[0, 0], -SWIGLU_LIMIT, SWIGLU_LIMIT)
        hmid = (u + 1.0) * (g * jax.nn.sigmoid(g * SWIGLU_ALPHA))
        _store_rows_as_tiles(y_ref, _dot(hmid.astype(BF16), wdb_ref[...]) + bd_ref[0, 0])

    @pl.when(b >= nu_ref[0])
    def _():
        y_ref[...] = jnp.zeros(y_ref.shape, y_ref.dtype)


def _experts(layer, block_e, n_used, xs, w_gate, b_gate, w_up, b_up, w_down, b_down):
    n_rows = xs.shape[0] // ROW_TILES
    d = w_gate.shape[-2]
    bm = BM_EXPERT
    nb = n_rows // bm
    dff = w_gate.shape[-1]
    wmap = lambda b, l, be, nu: (l[0], be[b], 0, 0)
    grid_spec = pltpu.PrefetchScalarGridSpec(
        num_scalar_prefetch=3,
        grid=(nb,),
        in_specs=[
            pl.BlockSpec((bm * ROW_TILES, LANES), lambda b, l, be, nu: (b, 0)),
            pl.BlockSpec((1, 1, d, dff), wmap),
            pl.BlockSpec((1, 1, 1, dff), wmap),
            pl.BlockSpec((1, 1, d, dff), wmap),
            pl.BlockSpec((1, 1, 1, dff), wmap),
            pl.BlockSpec((1, 1, dff, d), wmap),
            pl.BlockSpec((1, 1, 1, d), wmap),
        ],
        out_specs=pl.BlockSpec((bm * ROW_TILES, LANES), lambda b, l, be, nu: (b, 0)),
        scratch_shapes=[pltpu.VMEM((d, dff), BF16), pltpu.VMEM((d, dff), BF16), pltpu.VMEM((dff, d), BF16)],
    )
    depth, ne = b_gate.shape[:2]
    return pl.pallas_call(
        _expert_kernel,
        grid_spec=grid_spec,
        out_shape=jax.ShapeDtypeStruct((n_rows * ROW_TILES, LANES), F32),
        compiler_params=_cparams(("arbitrary",)),
        name="moe_experts",
    )(layer, block_e, n_used, xs, w_gate, b_gate.reshape(depth, ne, 1, dff), w_up,
      b_up.reshape(depth, ne, 1, dff), w_down, b_down.reshape(depth, ne, 1, d))


def _combine_kernel(dest_ref, gate_ref, x1_ref, g2_ref, lg_ref, lb_ref, yb_ref, o_ref, buf_ref, sem):
    tc = x1_ref.shape[0]

    def row_copy(src, k, r):
        return pltpu.make_async_copy(yb_ref.at[pl.ds(pl.multiple_of(src * ROW_TILES, ROW_TILES), ROW_TILES)],
                                     buf_ref.at[k, pl.ds(pl.multiple_of(r * ROW_TILES, ROW_TILES), ROW_TILES)], sem)

    def issue(r, carry):
        for k in range(TOP_K):
            row_copy(dest_ref[0, 0, r * TOP_K + k], k, r).start()
        return carry

    lax.fori_loop(0, tc, issue, 0, unroll=4)

    def drain(i, carry):
        row_copy(0, 0, 0).wait()
        return carry

    lax.fori_loop(0, tc * TOP_K, drain, 0, unroll=8)

    gates = gate_ref[...]
    f = gates[:, 0:1] * _load_rows_from_tiles(buf_ref.at[0])
    for k in range(1, TOP_K):
        f = f + gates[:, k:k + 1] * _load_rows_from_tiles(buf_ref.at[k])
    o_ref[...] = _layer_norm_rows(DN_ALPHA * x1_ref[...] + (1.0 + g2_ref[...]) * f,
                                  lg_ref[...], lb_ref[...])


def _combine(dest, gates, x1_2d, g2, ln_g, ln_b, yb, rows_per_mod):
    n, d = x1_2d.shape
    tc = min(TD_ROWS, n)
    nt = n // tc
    if rows_per_mod == 1:
        g2_spec = pl.BlockSpec((tc, d), lambda i: (i, 0))
        g2_in = g2
    else:
        tiles_per_mod = rows_per_mod // tc
        g2_spec = pl.BlockSpec((1, 1, d), lambda i: (i // tiles_per_mod, 0, 0))
        g2_in = g2.reshape(g2.shape[0], 1, d)
    kern = _combine_kernel if rows_per_mod == 1 else _combine_kernel_bcast
    return pl.pallas_call(
        kern,
        grid=(nt,),
        in_specs=[
            pl.BlockSpec((1, 1, tc * TOP_K), lambda i: (i, 0, 0), memory_space=pltpu.SMEM),
            pl.BlockSpec((tc, LANES), lambda i: (i, 0)),
            pl.BlockSpec((tc, d), lambda i: (i, 0)),
            g2_spec,
            pl.BlockSpec((1, d), lambda i: (0, 0)),
            pl.BlockSpec((1, d), lambda i: (0, 0)),
            pl.BlockSpec(memory_space=pl.ANY),
        ],
        out_specs=pl.BlockSpec((tc, d), lambda i: (i, 0)),
        scratch_shapes=[pltpu.VMEM((TOP_K, tc * ROW_TILES, LANES), F32), pltpu.SemaphoreType.DMA(())],
        out_shape=jax.ShapeDtypeStruct((n, d), F32),
        compiler_params=_cparams(("arbitrary",)),
        name="moe_combine",
    )(dest.reshape(nt, 1, tc * TOP_K), gates, x1_2d, g2_in, ln_g, ln_b, yb)


def _combine_kernel_bcast(dest_ref, gate_ref, x1_ref, g2_ref, lg_ref, lb_ref, yb_ref, o_ref, buf_ref, sem):
    _combine_kernel(dest_ref, gate_ref, x1_ref, g2_ref.at[0], lg_ref, lb_ref, yb_ref, o_ref, buf_ref, sem)


def _rope_tables(pos):
    half = QK_DIM // 2
    inv = ROPE_THETA ** (-jnp.arange(half, dtype=F32) / half)
    ang = pos.astype(F32)[:, None] * inv[None, :]
    cos = jnp.cos(ang)
    sin = jnp.sin(ang)
    reps = LANES // QK_DIM
    cos_t = jnp.tile(jnp.concatenate([cos, cos], axis=1), (1, reps))
    sin_t = jnp.tile(jnp.concatenate([-sin, sin], axis=1), (1, reps))
    return cos_t, sin_t


def _pad_rows(a, rows):
    return jnp.concatenate([a, jnp.zeros((rows - a.shape[0],) + a.shape[1:], a.dtype)], axis=0)


def kernel(x_prompt, x_sample, c_prompt, c_sample, cache_k, cache_v, state_conv, page_table, w_ada, b_ada, w_in, conv_w, lambda_q1, lambda_k1, lambda_q2, lambda_k2, subln_g, chunk_ln_g, chunk_ln_b, chunk_ws, chunk_bs, w_out, ln1_g, ln1_b, w_router, b_router, w_gate, b_gate, w_up, b_up, w_down, b_down, ln2_g, ln2_b):
    n_prompt, seq, d = x_prompt.shape
    n_dec, dec_seq, _ = x_sample.shape
    depth = w_in.shape[0]
    n_pool = cache_k.shape[1]
    n_pages = page_table.shape[1]
    past_len = n_pages * PAGE_SIZE
    ns = n_dec * dec_seq
    np_tok = n_prompt * seq
    assert dec_seq >= CONV_K - 1 and dec_seq <= SUBLANES and CHUNK % dec_seq == 0
    assert ns % CHUNK == 0 and seq % CHUNK == 0

    c_all = jnp.concatenate([c_prompt, c_sample], axis=0)
    m_rows = -(-c_all.shape[0] // SUBLANES) * SUBLANES
    mod_all = _ada_call(_pad_rows(c_all, m_rows), w_ada, b_ada)

    cos_p, sin_p = _rope_tables(jnp.arange(seq))
    cos_s, sin_s = _rope_tables(jnp.tile(past_len + jnp.arange(dec_seq), n_dec))

    cache_k4 = jnp.transpose(cache_k, (0, 1, 3, 4, 5, 2)).reshape(depth, n_pool, ATT_HEADS * 2 * QK_DIM, PAGE_SIZE)
    cache_v4 = jnp.transpose(cache_v, (0, 1, 3, 4, 2)).reshape(depth, n_pool, ATT_W, PAGE_SIZE)
    conv0 = jnp.zeros((n_prompt, CONV_K - 1, CONV_W), F32)
    lane = jnp.arange(LANES)
    tok_pos = jnp.arange(ns) % dec_seq

    nrow = 2 * dec_seq * ATT_HEADS
    r_map = jnp.arange(nrow) // (dec_seq * ATT_HEADS)
    r_q = (jnp.arange(nrow) // ATT_HEADS) % dec_seq
    r_h = jnp.arange(nrow) % ATT_HEADS
    col = jnp.arange(ATT_HEADS * 2 * QK_DIM)
    qbd_mask = ((col[None, :] // (2 * QK_DIM)) == r_h[:, None]) & (((col[None, :] // QK_DIM) % 2) == r_map[:, None])

    n_assign = (np_tok + ns) * TOP_K
    n_blocks = -(-n_assign // BM_EXPERT) + N_EXPERTS
    n_rows = n_blocks * BM_EXPERT

    y_p, y_s = x_prompt, x_sample.reshape(ns, d)
    xs = jnp.zeros((n_rows * ROW_TILES, LANES), F32)
    outs = {k: [] for k in ("kp", "vp", "cp", "chp", "ks", "vs", "cs", "chs")}
    for l in range(depth):
        layer = jnp.full((1,), l, jnp.int32)
        lam_init = 0.8 - 0.6 * math.exp(-0.3 * l)
        mod = mod_all[l]
        mp = mod[:n_prompt].reshape(n_prompt, 6, 1, d)
        ms = jnp.repeat(mod[n_prompt:n_prompt + n_dec], dec_seq, axis=0).reshape(ns, 6, d)
        w_in_b = w_in[l].astype(BF16)
        w_out_b = w_out[l].astype(BF16)
        cw = _pad_rows(conv_w[l], SUBLANES)
        lamp = jnp.zeros((SUBLANES, LANES), F32)
        lamp = lamp.at[0, :QK_DIM].set(lambda_q1[l]).at[1, :QK_DIM].set(lambda_k1[l])
        lamp = lamp.at[2, :QK_DIM].set(lambda_q2[l]).at[3, :QK_DIM].set(lambda_k2[l])
        lamp = lamp.at[4, :].set(lam_init)
        lng = chunk_ln_g[l].reshape(1, CHUNK_W)
        lnb = chunk_ln_b[l].reshape(1, CHUNK_W)
        ws = chunk_ws[l]
        bs = chunk_bs[l]
        wcat_p = jnp.concatenate([jnp.tril(ws[g]) for g in range(CHUNK_GROUPS)], axis=1).astype(BF16)
        bsf_p = jnp.repeat(bs.T, CHUNK_GW, axis=1)
        eye = jnp.eye(CHUNK // dec_seq, dtype=F32)
        wcat_s = jnp.concatenate([jnp.kron(eye, jnp.tril(ws[g, :dec_seq, :dec_seq]))
                                  for g in range(CHUNK_GROUPS)], axis=1).astype(BF16)
        bsf_s = jnp.tile(jnp.repeat(bs[:, :dec_seq].T, CHUNK_GW, axis=1), (CHUNK // dec_seq, 1))
        g_sub = subln_g[l]
        wr_pad = jnp.concatenate([w_router[l], jnp.zeros((d, LANES - N_EXPERTS), F32)], axis=1)
        br_pad = jnp.concatenate([b_router[l], jnp.full((LANES - N_EXPERTS,), NEG_BIG, F32)]).reshape(1, LANES)
        l1g, l1b = ln1_g[l].reshape(1, d), ln1_b[l].reshape(1, d)
        l2g, l2b = ln2_g[l].reshape(1, d), ln2_b[l].reshape(1, d)

        (q_p, kf_p, kb_p, vf_p, vb_p, co_p, cho_p, cst_p, chst_p) = _inproj_prompt(
            y_p, mp[:, 1], mp[:, 0], w_in_b, cos_p, sin_p, cw, conv0, lng, lnb, wcat_p, bsf_p)
        att_p = _attn_prompt(q_p, kb_p, vb_p, lamp, jnp.broadcast_to(g_sub[:, None], (V_DIM, min(TQ_ATT, seq))))
        zero_carry = jnp.zeros((SUBLANES, LANES), F32)
        x1_p, h2_p, route_p, gate_p, cnt_p = _outproj(
            y_p, co_p, att_p, cho_p, mp[:, 2], mp[:, 4], mp[:, 3], w_out_b, l1g, l1b, wr_pad, br_pad,
            zero_carry, True)

        st = state_conv[l]
        prev1 = jnp.repeat(st[:, 1], dec_seq, axis=0)
        prev2 = jnp.where((tok_pos == 0)[:, None], jnp.repeat(st[:, 0], dec_seq, axis=0), prev1)
        (q_s, kf_s, kb_s, vf_s, vb_s, co_s, cho_s, z_s, vn_s) = _inproj_sample(
            y_s, ms[:, 1], ms[:, 0], w_in_b, cos_s, sin_s, cw, prev1, prev2, lng, lnb, wcat_s, bsf_s, dec_seq)
        q3 = q_s.reshape(n_dec, dec_seq, -1)
        qbd = jnp.where(qbd_mask[None], q3[:, r_q, :], jnp.zeros((), BF16))
        pad3 = lambda a: jnp.concatenate(
            [a.reshape(n_dec, dec_seq, -1),
             jnp.zeros((n_dec, NEW_PAD - dec_seq, a.shape[-1]), a.dtype)], axis=1)
        att_s = _attn_sample(layer, page_table, cache_k4, cache_v4, qbd, pad3(kf_s), pad3(vf_s), lamp,
                             jnp.tile(g_sub, ATT_HEADS).reshape(1, ATT_W), dec_seq)
        x1_s, h2_s, route_s, gate_s, cnt_all = _outproj(
            y_s, co_s, att_s.reshape(ns, ATT_W).astype(BF16), cho_s, ms[:, 2], ms[:, 4], ms[:, 3],
            w_out_b, l1g, l1b, wr_pad, br_pad, cnt_p, False)

        counts = cnt_all[0, :N_EXPERTS].astype(jnp.int32)
        pcounts = ((counts + BM_EXPERT - 1) // BM_EXPERT) * BM_EXPERT
        pends = jnp.cumsum(pcounts)
        pstart = (pends - pcounts).astype(jnp.int32)
        blk_row = jnp.arange(n_blocks, dtype=jnp.int32) * BM_EXPERT
        block_e = jnp.minimum(jnp.sum(blk_row[:, None] >= pends[None, :], axis=1), N_EXPERTS - 1).astype(jnp.int32)
        n_used = (pends[-1] // BM_EXPERT).astype(jnp.int32).reshape(1)
        def dest_rows(route):
            idx, rank = route[:, :TOP_K], route[:, TOP_K:2 * TOP_K]
            hot = idx[:, :, None] == jnp.arange(N_EXPERTS, dtype=jnp.int32)[None, None, :]
            return rank + jnp.sum(jnp.where(hot, pstart[None, None, :], 0), axis=-1)

        dest_p = dest_rows(route_p.reshape(np_tok, LANES))
        dest_s = dest_rows(route_s)
        xs = _dispatch(dest_p, h2_p.reshape(np_tok * ROW_TILES, LANES), xs)
        xs = _dispatch(dest_s, h2_s, xs)
        yb = _experts(layer, block_e, n_used, xs, w_gate, b_gate, w_up, b_up, w_down, b_down)
        y_p = _combine(dest_p, gate_p.reshape(np_tok, LANES), x1_p.reshape(np_tok, d),
                       mod[:n_prompt, 5 * d:], l2g, l2b, yb, seq).reshape(n_prompt, seq, d)
        y_s = _combine(dest_s, gate_s, x1_s, ms[:, 5], l2g, l2b, yb, 1)

        outs["kp"].append(jnp.transpose(kf_p.reshape(n_prompt, ATT_HEADS, 2, QK_DIM, seq), (0, 4, 1, 2, 3)))
        outs["vp"].append(jnp.transpose(vf_p.reshape(n_prompt, ATT_HEADS, V_DIM, seq), (0, 3, 1, 2)))
        outs["cp"].append(cst_p)
        outs["chp"].append(chst_p)
        outs["ks"].append(kf_s.reshape(n_dec, dec_seq, ATT_HEADS, 2, QK_DIM))
        outs["vs"].append(vf_s.reshape(n_dec, dec_seq, ATT_HEADS, V_DIM))
        outs["cs"].append(z_s.reshape(n_dec, dec_seq, CONV_W)[:, dec_seq - (CONV_K - 1):])
        outs["chs"].append(vn_s.reshape(n_dec, dec_seq, CHUNK_W))

    st_ = lambda k: jnp.stack(outs[k])
    return (y_p, y_s.reshape(n_dec, dec_seq, d), st_("kp"), st_("vp"), st_("cp"), st_("chp"),
            st_("ks"), st_("vs"), st_("cs"), st_("chs"))
```

```python
import functools
import math

import jax
import jax.numpy as jnp
from jax import lax
from jax.experimental import pallas as pl
from jax.experimental.pallas import tpu as pltpu

F32 = jnp.float32
BF16 = jnp.bfloat16

D_MODEL = 1024
CONV_W = 256
CONV_K = 3
ATT_HEADS = 8
QK_DIM = 32
V_DIM = 64
ATT_W = ATT_HEADS * V_DIM
CHUNK_GROUPS = 4
CHUNK_GW = 64
CHUNK_W = CHUNK_GROUPS * CHUNK_GW
CHUNK = 128
O_CH = 0
O_Q = 3 * CONV_W
O_K = O_Q + ATT_HEADS * 2 * QK_DIM
O_V = O_K + ATT_HEADS * 2 * QK_DIM
O_U = O_V + ATT_W
O_SV = O_U + CHUNK_W
IN_W = O_SV + CHUNK_W
N_EXPERTS = 32
TOP_K = 4
D_FF = 1024
SWIGLU_ALPHA = 1.702
SWIGLU_LIMIT = 7.0
PAGE_SIZE = 128
ROPE_THETA = 10000.0
LN_EPS = 1e-5
DEPTH = 4
DN_ALPHA = (2 * DEPTH) ** 0.25
Q_SCALE = QK_DIM ** -0.5 * math.log2(math.e)

LANES = 128
SUBLANES = 8
ROW_TILES = D_MODEL // LANES
VMEM_BYTES_V7X = 64 * 1024 * 1024

TM_PROJ = 512
TQ_ATT = 512
TK_ATT = 512
PAGES_PER_STEP = 8
BM_EXPERT = 256
TD_ROWS = 256
NEW_PAD = PAGE_SIZE
NEG_BIG = -1e30
VMEM_LIMIT = 56 * 1024 * 1024


def _cparams(sem):
    return pltpu.CompilerParams(dimension_semantics=sem, vmem_limit_bytes=VMEM_LIMIT)


def _split_bf16(a):
    hi = a.astype(BF16)
    lo = (a - hi.astype(F32)).astype(BF16)
    return hi, lo


def _dot(a, b):
    return jnp.dot(a, b, preferred_element_type=F32)


def _dot_split_lhs(a, b_bf16):
    hi, lo = _split_bf16(a)
    return _dot(hi, b_bf16) + _dot(lo, b_bf16)


def _dot3(a, b):
    ah, al = _split_bf16(a)
    bh, bl = _split_bf16(b)
    return _dot(ah, bh) + (_dot(al, bh) + _dot(ah, bl))


def _store_rows_as_tiles(ref, val):
    rows = val.shape[0]
    for j in range(ROW_TILES):
        ref[pl.ds(j, rows, stride=ROW_TILES), :] = val[:, j * LANES:(j + 1) * LANES]


def _load_rows_from_tiles(ref):
    rows = ref.shape[0] // ROW_TILES
    return jnp.concatenate([ref[pl.ds(j, rows, stride=ROW_TILES), :] for j in range(ROW_TILES)], axis=1)


def _layer_norm_rows(v, g, b):
    mu = jnp.mean(v, axis=-1, keepdims=True)
    d = v - mu
    var = jnp.mean(d * d, axis=-1, keepdims=True)
    return d * lax.rsqrt(var + LN_EPS) * g + b


def _ada_kernel(c_ref, w_ref, b_ref, o_ref):
    o_ref[0] = _dot3(c_ref[...], w_ref[0]) + b_ref[0]


def _ada_call(c_all, w_ada, b_ada):
    depth, d, n6 = w_ada.shape
    m = c_all.shape[0]
    tn = 1024
    return pl.pallas_call(
        _ada_kernel,
        grid=(depth, n6 // tn),
        in_specs=[
            pl.BlockSpec((m, d), lambda l, j: (0, 0)),
            pl.BlockSpec((1, d, tn), lambda l, j: (l, 0, j)),
            pl.BlockSpec((1, 1, tn), lambda l, j: (l, 0, j)),
        ],
        out_specs=pl.BlockSpec((1, m, tn), lambda l, j: (l, 0, j)),
        out_shape=jax.ShapeDtypeStruct((depth, m, n6), F32),
        compiler_params=_cparams(("arbitrary", "arbitrary")),
        name="ada",
    )(c_all, w_ada, b_ada.reshape(depth, 1, n6))


def _rope_chunks(p, cos_t, sin_t, scale):
    lane = lax.broadcasted_iota(jnp.int32, (1, LANES), 1)
    first_half = (lane % QK_DIM) < (QK_DIM // 2)
    outs = []
    for c in range(p.shape[1] // LANES):
        xc = p[:, c * LANES:(c + 1) * LANES]
        up = pltpu.roll(xc, LANES - QK_DIM // 2, 1)
        dn = pltpu.roll(xc, QK_DIM // 2, 1)
        r = xc * cos_t + jnp.where(first_half, up, dn) * sin_t
        if scale is not None:
            r = r * scale
        outs.append(r)
    return jnp.concatenate(outs, axis=1)


def _group_norm(sv, g, b):
    ri = lax.broadcasted_iota(jnp.int32, (CHUNK_W, CHUNK_W), 0) // CHUNK_GW
    ci = lax.broadcasted_iota(jnp.int32, (CHUNK_W, CHUNK_W), 1) // CHUNK_GW
    avg = jnp.where(ri == ci, 1.0 / CHUNK_GW, 0.0).astype(BF16)
    mu = _dot_split_lhs(sv, avg)
    d = sv - mu
    var = _dot_split_lhs(d * d, avg)
    return d * lax.rsqrt(var + LN_EPS) * g + b


def _chunk_mix(vn, wcat_ref, bsf_ref):
    gid = lax.broadcasted_iota(jnp.int32, (1, CHUNK_W), 1) // CHUNK_GW
    vb = vn.astype(BF16)
    zero = jnp.zeros_like(vb[:CHUNK])
    outs = []
    for c in range(vn.shape[0] // CHUNK):
        vc = vb[c * CHUNK:(c + 1) * CHUNK]
        stack = jnp.concatenate([jnp.where(gid == g, vc, zero) for g in range(CHUNK_GROUPS)], axis=0)
        outs.append(_dot(wcat_ref[...], stack) + bsf_ref[...])
    return jnp.concatenate(outs, axis=0)


def _inproj_kernel(*refs, seq_mode, dec_seq):
    if seq_mode:
        (x_ref, sc_ref, sh_ref, w_ref, cos_ref, sin_ref, cw_ref, cprev_ref, lng_ref, lnb_ref,
         wcat_ref, bsf_ref,
         q_ref, kf_ref, kb_ref, vf_ref, vb_ref, co_ref, cho_ref, cst_ref, chst_ref, zc_ref) = refs
        x = x_ref[0]
        sc = sc_ref[0]
        sh = sh_ref[0]
    else:
        (x_ref, sc_ref, sh_ref, w_ref, cos_ref, sin_ref, cw_ref, p1_ref, p2_ref, lng_ref, lnb_ref,
         wcat_ref, bsf_ref,
         q_ref, kf_ref, kb_ref, vf_ref, vb_ref, co_ref, cho_ref, cst_ref, chst_ref) = refs
        x = x_ref[...]
        sc = sc_ref[...]
        sh = sh_ref[...]
    tm = x.shape[0]
    h = (x * (1.0 + sc) + sh).astype(BF16)

    pc = _dot(h, w_ref[:, O_CH:O_Q])
    conv_in, gate_b, gate_c = pc[:, :CONV_W], pc[:, CONV_W:2 * CONV_W], pc[:, 2 * CONV_W:]
    z = gate_c * conv_in
    row = lax.broadcasted_iota(jnp.int32, (tm, 1), 0)
    z1 = pltpu.roll(z, 1, 0)
    z2 = pltpu.roll(z, 2, 0)
    if seq_mode:
        @pl.when(pl.program_id(1) == 0)
        def _():
            zc_ref[0:2, :] = cprev_ref[0]
        c0 = zc_ref[0:1, :]
        c1 = zc_ref[1:2, :]
        z1 = jnp.where(row == 0, c1, z1)
        z2 = jnp.where(row == 0, c0, jnp.where(row == 1, c1, z2))
        zc_ref[0:2, :] = z[tm - 2:tm, :]
        cst_ref[0] = z[tm - 2:tm, :]
    else:
        pos_in_seq = row % dec_seq
        z1 = jnp.where(pos_in_seq >= 1, z1, p1_ref[...])
        z2 = jnp.where(pos_in_seq >= 2, z2, p2_ref[...])
        cst_ref[...] = z
    y = cw_ref[0:1, :] * z2 + cw_ref[1:2, :] * z1 + cw_ref[2:3, :] * z
    conv_out = (gate_b * y).astype(BF16)

    pqk = _dot(h, w_ref[:, O_Q:O_V])
    cos_t = cos_ref[...]
    sin_t = sin_ref[...]
    nq = O_K - O_Q
    qr = _rope_chunks(pqk[:, :nq], cos_t, sin_t, Q_SCALE)
    kr = _rope_chunks(pqk[:, nq:], cos_t, sin_t, None)
    pv = _dot(h, w_ref[:, O_V:O_U])

    pu = _dot(h, w_ref[:, O_U:IN_W])
    u, sv = pu[:, :CHUNK_W], pu[:, CHUNK_W:]
    vn = _group_norm(sv, lng_ref[...], lnb_ref[...])
    chunk_out = (u * _chunk_mix(vn, wcat_ref, bsf_ref)).astype(BF16)

    if seq_mode:
        q_ref[0] = qr.T.astype(BF16)
        kf_ref[0] = kr.T
        kb_ref[0] = kr.astype(BF16)
        vt = pv.T
        vf_ref[0] = vt
        vtb = vt.astype(BF16)
        for c in range(tm // TK_ATT):
            vb_ref[0, c] = vtb[:, c * TK_ATT:(c + 1) * TK_ATT]
        co_ref[0] = conv_out
        cho_ref[0] = chunk_out
        chst_ref[0] = vn[tm - CHUNK:tm, :]
    else:
        q_ref[...] = qr.astype(BF16)
        kf_ref[...] = kr
        kb_ref[...] = kr.astype(BF16)
        vf_ref[...] = pv
        vb_ref[...] = pv.astype(BF16)
        co_ref[...] = conv_out
        cho_ref[...] = chunk_out
        chst_ref[...] = vn


def _inproj_prompt(x, sc1, sh1, w_in_b, cos_t, sin_t, conv_w, conv_prev, ln_g, ln_b, wcat, bsf):
    b_, t_, d = x.shape
    tm = min(TM_PROJ, t_)
    nt = t_ // tm
    const2 = lambda b, i: (0, 0)
    tok3 = lambda b, i: (b, i, 0)
    tokT = lambda b, i: (b, 0, i)
    bat3 = lambda b, i: (b, 0, 0)
    qk_w = ATT_HEADS * 2 * QK_DIM
    outs = pl.pallas_call(
        functools.partial(_inproj_kernel, seq_mode=True, dec_seq=0),
        grid=(b_, nt),
        in_specs=[
            pl.BlockSpec((1, tm, d), tok3),
            pl.BlockSpec((1, 1, d), bat3),
            pl.BlockSpec((1, 1, d), bat3),
            pl.BlockSpec((d, IN_W), const2),
            pl.BlockSpec((tm, LANES), lambda b, i: (i, 0)),
            pl.BlockSpec((tm, LANES), lambda b, i: (i, 0)),
            pl.BlockSpec((SUBLANES, CONV_W), const2),
            pl.BlockSpec((1, CONV_K - 1, CONV_W), bat3),
            pl.BlockSpec((1, CHUNK_W), const2),
            pl.BlockSpec((1, CHUNK_W), const2),
            pl.BlockSpec((CHUNK, CHUNK_GROUPS * CHUNK), const2),
            pl.BlockSpec((CHUNK, CHUNK_W), const2),
        ],
        out_specs=[
            pl.BlockSpec((1, qk_w, tm), tokT),
            pl.BlockSpec((1, qk_w, tm), tokT),
            pl.BlockSpec((1, tm, qk_w), tok3),
            pl.BlockSpec((1, ATT_W, tm), tokT),
            pl.BlockSpec((1, tm // TK_ATT, ATT_W, TK_ATT), lambda b, i: (b, i, 0, 0)),
            pl.BlockSpec((1, tm, CONV_W), tok3),
            pl.BlockSpec((1, tm, CHUNK_W), tok3),
            pl.BlockSpec((1, CONV_K - 1, CONV_W), bat3),
            pl.BlockSpec((1, CHUNK, CHUNK_W), bat3),
        ],
        out_shape=[
            jax.ShapeDtypeStruct((b_, qk_w, t_), BF16),
            jax.ShapeDtypeStruct((b_, qk_w, t_), F32),
            jax.ShapeDtypeStruct((b_, t_, qk_w), BF16),
            jax.ShapeDtypeStruct((b_, ATT_W, t_), F32),
            jax.ShapeDtypeStruct((b_, t_ // TK_ATT, ATT_W, TK_ATT), BF16),
            jax.ShapeDtypeStruct((b_, t_, CONV_W), BF16),
            jax.ShapeDtypeStruct((b_, t_, CHUNK_W), BF16),
            jax.ShapeDtypeStruct((b_, CONV_K - 1, CONV_W), F32),
            jax.ShapeDtypeStruct((b_, CHUNK, CHUNK_W), F32),
        ],
        scratch_shapes=[pltpu.VMEM((SUBLANES, CONV_W), F32)],
        compiler_params=_cparams(("arbitrary", "arbitrary")),
        name="inproj_prompt",
    )(x, sc1, sh1, w_in_b, cos_t, sin_t, conv_w, conv_prev, ln_g, ln_b, wcat, bsf)
    return outs


def _inproj_sample(x2d, sc1, sh1, w_in_b, cos_t, sin_t, conv_w, prev1, prev2, ln_g, ln_b, wcat, bsf,
                   dec_seq):
    n, d = x2d.shape
    qk_w = ATT_HEADS * 2 * QK_DIM
    full = lambda shape: pl.BlockSpec(shape, lambda i: tuple(0 for _ in shape))
    outs = pl.pallas_call(
        functools.partial(_inproj_kernel, seq_mode=False, dec_seq=dec_seq),
        grid=(1,),
        in_specs=[
            full((n, d)), full((n, d)), full((n, d)), full((d, IN_W)),
            full((n, LANES)), full((n, LANES)), full((SUBLANES, CONV_W)),
            full((n, CONV_W)), full((n, CONV_W)), full((1, CHUNK_W)), full((1, CHUNK_W)),
            full((CHUNK, CHUNK_GROUPS * CHUNK)), full((CHUNK, CHUNK_W)),
        ],
        out_specs=[
            full((n, qk_w)), full((n, qk_w)), full((n, qk_w)), full((n, ATT_W)), full((n, ATT_W)),
            full((n, CONV_W)), full((n, CHUNK_W)), full((n, CONV_W)), full((n, CHUNK_W)),
        ],
        out_shape=[
            jax.ShapeDtypeStruct((n, qk_w), BF16),
            jax.ShapeDtypeStruct((n, qk_w), F32),
            jax.ShapeDtypeStruct((n, qk_w), BF16),
            jax.ShapeDtypeStruct((n, ATT_W), F32),
            jax.ShapeDtypeStruct((n, ATT_W), BF16),
            jax.ShapeDtypeStruct((n, CONV_W), BF16),
            jax.ShapeDtypeStruct((n, CHUNK_W), BF16),
            jax.ShapeDtypeStruct((n, CONV_W), F32),
            jax.ShapeDtypeStruct((n, CHUNK_W), F32),
        ],
        compiler_params=_cparams(("arbitrary",)),
        name="inproj_sample",
    )(x2d, sc1, sh1, w_in_b, cos_t, sin_t, conv_w, prev1, prev2, ln_g, ln_b, wcat, bsf)
    return outs


def _lambda_from(lam_ref):
    lp = lam_ref[...]
    s1 = jnp.sum(lp[0:1, :] * lp[1:2, :], axis=1, keepdims=True)
    s2 = jnp.sum(lp[2:3, :] * lp[3:4, :], axis=1, keepdims=True)
    lam_init = lp[4:5, 0:1]
    return jnp.exp(s1) - jnp.exp(s2) + lam_init, lam_init


L_ROWS = 16


def _attn_prompt_kernel(qt_ref, k_ref, vt_ref, lam_ref, g_ref, o_ref, qm_ref, m_ref, acc_ref, s0_ref, s1_ref):
    tq = qt_ref.shape[2]
    tk = vt_ref.shape[3]
    qi = pl.program_id(2)
    n_hm = 2 * 2

    qt = qt_ref[0]
    row_hm = lax.broadcasted_iota(jnp.int32, (LANES, 1), 0) // QK_DIM
    zero = jnp.zeros_like(qt)
    for hm in range(n_hm):
        qm_ref[hm] = jnp.where(row_hm == hm, qt, zero)
    m_ref[...] = jnp.full(m_ref.shape, NEG_BIG, F32)
    acc_ref[...] = jnp.zeros(acc_ref.shape, F32)

    s_bufs = (s0_ref, s1_ref)

    def scores(ki, slot):
        start = pl.multiple_of(ki * tk, tk)
        kt = k_ref[0, pl.ds(start, tk), :]
        for hm in range(n_hm):
            s_bufs[slot][hm] = _dot(kt, qm_ref[hm])

    def consume(ki, slot, masked):
        vt = vt_ref[0, ki]
        ones = jnp.ones((L_ROWS, tk), BF16)
        m_old = [m_ref[hm] for hm in range(n_hm)]
        acc_old = [acc_ref[hm] for hm in range(n_hm)]
        lhs = [jnp.concatenate([vt[h * V_DIM:(h + 1) * V_DIM, :], ones], axis=0) for h in range(2)]
        s = [s_bufs[slot][hm] for hm in range(n_hm)]
        if masked:
            key = lax.broadcasted_iota(jnp.int32, (tk, tq), 0)
            qry = lax.broadcasted_iota(jnp.int32, (tk, tq), 1)
            valid = key <= qry
            s = [jnp.where(valid, s_, NEG_BIG) for s_ in s]
        m_new = [jnp.maximum(m_old[hm], jnp.max(s[hm], axis=0, keepdims=True)) for hm in range(n_hm)]
        p = [jnp.exp2(s[hm] - m_new[hm]).astype(BF16) for hm in range(n_hm)]
        pv = [_dot(lhs[hm // 2], p[hm]) for hm in range(n_hm)]
        for hm in range(n_hm):
            acc_ref[hm] = jnp.exp2(m_old[hm] - m_new[hm]) * acc_old[hm] + pv[hm]
            m_ref[hm] = m_new[hm]

    scores(0, 0)

    def pair(j, carry):
        scores(2 * j + 1, 1)
        consume(2 * j, 0, False)
        scores(2 * j + 2, 0)
        consume(2 * j + 1, 1, False)
        return carry

    lax.fori_loop(0, qi // 2, pair, 0)

    @pl.when(qi % 2 == 0)
    def _():
        consume(qi, 0, True)

    @pl.when(qi % 2 == 1)
    def _():
        scores(qi, 1)
        consume(qi - 1, 0, False)
        consume(qi, 1, True)

    lam, lam_init = _lambda_from(lam_ref)
    halves = []
    for h in range(2):
        a0 = acc_ref[2 * h]
        a1 = acc_ref[2 * h + 1]
        o0 = a0[:V_DIM] * (1.0 / a0[V_DIM:V_DIM + 1])
        o1 = a1[:V_DIM] * (1.0 / a1[V_DIM:V_DIM + 1])
        d = o0 - lam * o1
        ms = jnp.sum(d * d, axis=0, keepdims=True) * (1.0 / V_DIM)
        halves.append(d * lax.rsqrt(ms + LN_EPS) * g_ref[...] * (1.0 - lam_init))
    o_ref[0] = jnp.concatenate(halves, axis=0).T.astype(o_ref.dtype)


def _attn_prompt(qt, k, vt4, lamp, g_col):
    b_, t_, _ = k.shape
    tk = vt4.shape[3]
    tq = g_col.shape[1]
    assert tq == tk, "the causal diagonal is handled as one square tile"
    nq = t_ // tq
    npair = ATT_HEADS // 2
    return pl.pallas_call(
        _attn_prompt_kernel,
        grid=(b_, npair, nq),
        in_specs=[
            pl.BlockSpec((1, LANES, tq), lambda b, h, i: (b, h, i)),
            pl.BlockSpec((1, t_, LANES), lambda b, h, i: (b, 0, h)),
            pl.BlockSpec((1, t_ // tk, LANES, tk), lambda b, h, i: (b, 0, h, 0)),
            pl.BlockSpec((SUBLANES, LANES), lambda b, h, i: (0, 0)),
            pl.BlockSpec((V_DIM, tq), lambda b, h, i: (0, 0)),
        ],
        out_specs=pl.BlockSpec((1, tq, LANES), lambda b, h, i: (b, i, h)),
        out_shape=jax.ShapeDtypeStruct((b_, t_, ATT_W), BF16),
        scratch_shapes=[
            pltpu.VMEM((4, LANES, tq), BF16),
            pltpu.VMEM((4, 1, tq), F32),
            pltpu.VMEM((4, V_DIM + L_ROWS, tq), F32),
            pltpu.VMEM((4, tk, tq), F32),
            pltpu.VMEM((4, tk, tq), F32),
        ],
        compiler_params=_cparams(("arbitrary", "arbitrary", "arbitrary")),
        name="attn_prompt",
    )(qt, k, vt4, lamp, g_col)


def _attn_sample_kernel(l_ref, pt_ref, *refs, pps, dec_seq):
    k_refs = refs[:pps]
    v_refs = refs[pps:2 * pps]
    qbd_ref, kn_ref, vn_ref, lam_ref, g_ref, o_ref, m_ref, l_acc_ref, acc_ref = refs[2 * pps:]
    j = pl.program_id(1)
    nrow = qbd_ref.shape[1]
    half = nrow // 2

    @pl.when(j == 0)
    def _():
        m_ref[...] = jnp.full(m_ref.shape, NEG_BIG, F32)
        l_acc_ref[...] = jnp.zeros(l_acc_ref.shape, F32)
        acc_ref[...] = jnp.zeros(acc_ref.shape, F32)

    qbd = qbd_ref[0]

    nt = (((1,), (1,)), ((), ()))

    def online(s, vals, vals_transposed):
        m_old = m_ref[...]
        m_new = jnp.maximum(m_old, jnp.max(s, axis=1, keepdims=True))
        alpha = jnp.exp2(m_old - m_new)
        p = jnp.exp2(s - m_new)
        m_ref[...] = m_new
        l_acc_ref[...] = alpha * l_acc_ref[...] + jnp.sum(p, axis=1, keepdims=True)
        pb = p.astype(BF16)
        pv = None
        for r, val in enumerate(vals):
            pr = pb[:, r * PAGE_SIZE:(r + 1) * PAGE_SIZE]
            if vals_transposed:
                t = lax.dot_general(pr, val, nt, preferred_element_type=F32)
            else:
                t = _dot(pr, val)
            pv = t if pv is None else pv + t
        acc_ref[...] = alpha * acc_ref[...] + pv

    s_parts = [_dot(qbd, k_refs[r][0, 0].astype(BF16)) for r in range(pps)]
    online(jnp.concatenate(s_parts, axis=1), [v_refs[r][0, 0].astype(BF16) for r in range(pps)], True)

    @pl.when(j == pl.num_programs(1) - 1)
    def _():
        kn = kn_ref[0].astype(BF16)
        s = lax.dot_general(qbd, kn, nt, preferred_element_type=F32)
        r_ = lax.broadcasted_iota(jnp.int32, s.shape, 0)
        c_ = lax.broadcasted_iota(jnp.int32, s.shape, 1)
        qpos = (r_ % half) // ATT_HEADS
        s = jnp.where(c_ <= qpos, s, NEG_BIG)
        online(s, [vn_ref[0].astype(BF16)], False)

        lam, lam_init = _lambda_from(lam_ref)
        o = acc_ref[...] / l_acc_ref[...]
        d = o[:half] - lam * o[half:]
        rr = lax.broadcasted_iota(jnp.int32, d.shape, 0) % ATT_HEADS
        cc = lax.broadcasted_iota(jnp.int32, d.shape, 1) // V_DIM
        d = jnp.where(rr == cc, d, 0.0)
        ms = jnp.sum(d * d, axis=1, keepdims=True) * (1.0 / V_DIM)
        dn = d * lax.rsqrt(ms + LN_EPS) * g_ref[...] * (1.0 - lam_init)
        o_ref[0] = jnp.sum(dn.reshape(dec_seq, ATT_HEADS, ATT_W), axis=1)


def _attn_sample(layer, page_table, cache_k4, cache_v4, qbd, kn_pad, vn_pad, lamp, g8, dec_seq):
    n_dec, n_pages = page_table.shape
    pps = PAGES_PER_STEP
    while n_pages % pps:
        pps //= 2
    nrow = qbd.shape[1]
    qkw = qbd.shape[2]

    def page_map(r):
        return lambda b, j, l_ref, pt_ref: (l_ref[0], pt_ref[b * n_pages + j * pps + r], 0, 0)

    seq3 = lambda b, j, l_ref, pt_ref: (b, 0, 0)
    const2 = lambda b, j, l_ref, pt_ref: (0, 0)
    in_specs = ([pl.BlockSpec((1, 1, qkw, PAGE_SIZE), page_map(r)) for r in range(pps)]
                + [pl.BlockSpec((1, 1, ATT_W, PAGE_SIZE), page_map(r)) for r in range(pps)]
                + [pl.BlockSpec((1, nrow, qkw), seq3),
                   pl.BlockSpec((1, NEW_PAD, qkw), seq3),
                   pl.BlockSpec((1, NEW_PAD, ATT_W), seq3),
                   pl.BlockSpec((SUBLANES, LANES), const2),
                   pl.BlockSpec((1, ATT_W), const2)])
    grid_spec = pltpu.PrefetchScalarGridSpec(
        num_scalar_prefetch=2,
        grid=(n_dec, n_pages // pps),
        in_specs=in_specs,
        out_specs=pl.BlockSpec((1, dec_seq, ATT_W), seq3),
        scratch_shapes=[
            pltpu.VMEM((nrow, 1), F32),
            pltpu.VMEM((nrow, 1), F32),
            pltpu.VMEM((nrow, ATT_W), F32),
        ],
    )
    return pl.pallas_call(
        functools.partial(_attn_sample_kernel, pps=pps, dec_seq=dec_seq),
        grid_spec=grid_spec,
        out_shape=jax.ShapeDtypeStruct((n_dec, dec_seq, ATT_W), F32),
        compiler_params=_cparams(("arbitrary", "arbitrary")),
        name="attn_sample",
    )(layer, page_table.reshape(-1), *([cache_k4] * pps), *([cache_v4] * pps),
      qbd, kn_pad, vn_pad, lamp, g8)


def _outproj_kernel(x_ref, co_ref, att_ref, cho_ref, g1_ref, sc2_ref, sh2_ref, wo_ref, lg_ref, lb_ref,
                    wr_ref, br_ref, cin_ref,
                    x1_ref, h2_ref, route_ref, gate_ref, cnt_ref, carry_ref, *, batched):
    first = (pl.program_id(0) == 0) & (pl.program_id(1) == 0)

    @pl.when(first)
    def _():
        carry_ref[...] = cin_ref[...]

    if batched:
        x, co, att, cho = x_ref[0], co_ref[0], att_ref[0], cho_ref[0]
        g1, sc2, sh2 = g1_ref[0], sc2_ref[0], sh2_ref[0]
    else:
        x, co, att, cho = x_ref[...], co_ref[...], att_ref[...], cho_ref[...]
        g1, sc2, sh2 = g1_ref[...], sc2_ref[...], sh2_ref[...]
    tm = x.shape[0]
    mix = jnp.concatenate([co, att, cho], axis=1)
    y = _dot(mix, wo_ref[...])
    x1 = _layer_norm_rows(DN_ALPHA * x + (1.0 + g1) * y, lg_ref[...], lb_ref[...])
    h2 = x1 * (1.0 + sc2) + sh2

    logits = _dot3(h2, wr_ref[...]) + br_ref[...]
    lane = lax.broadcasted_iota(jnp.int32, (1, LANES), 1)
    lane_f = lane.astype(F32)
    work = logits
    vals, hots, idxs = [], [], []
    for _ in range(TOP_K):
        mx = jnp.max(work, axis=1, keepdims=True)
        idx = jnp.min(jnp.where(work == mx, lane_f, float(LANES)), axis=1, keepdims=True)
        hot = lane_f == idx
        work = jnp.where(hot, -3e38, work)
        vals.append(mx)
        hots.append(hot)
        idxs.append(idx)
    exps = [jnp.exp(v - vals[0]) for v in vals]
    inv = 1.0 / (exps[0] + exps[1] + exps[2] + exps[3])

    any_hot = hots[0] | hots[1] | hots[2] | hots[3]
    a_b = jnp.where(any_hot, 1.0, 0.0).astype(BF16)
    ri = lax.broadcasted_iota(jnp.int32, (tm, tm), 0)
    ci = lax.broadcasted_iota(jnp.int32, (tm, tm), 1)
    stril = jnp.where(ci < ri, 1.0, 0.0).astype(BF16)
    before = _dot(stril, a_b) + carry_ref[0:1, :]
    new_carry = carry_ref[0:1, :] + jnp.sum(a_b.astype(F32), axis=0, keepdims=True)
    carry_ref[0:1, :] = new_carry

    route = jnp.zeros((tm, LANES), F32)
    gates = jnp.zeros((tm, LANES), F32)
    for k in range(TOP_K):
        rank_k = jnp.sum(jnp.where(hots[k], before, 0.0), axis=1, keepdims=True)
        route = jnp.where(lane == k, idxs[k], route)
        route = jnp.where(lane == TOP_K + k, rank_k, route)
        gates = jnp.where(lane == k, exps[k] * inv, gates)

    if batched:
        x1_ref[0] = x1
        _store_rows_as_tiles(h2_ref.at[0], h2)
        route_ref[0] = route.astype(jnp.int32)
        gate_ref[0] = gates
    else:
        x1_ref[...] = x1
        _store_rows_as_tiles(h2_ref, h2)
        route_ref[...] = route.astype(jnp.int32)
        gate_ref[...] = gates
    cnt_ref[...] = jnp.broadcast_to(new_carry, cnt_ref.shape)


def _outproj(x, co, att, cho, g1, sc2, sh2, w_out_b, ln_g, ln_b, wr_pad, br_pad, carry_in, batched):
    if batched:
        b_, t_, d = x.shape
        tm = min(TM_PROJ, t_)
        grid = (b_, t_ // tm)
        tok = lambda w: pl.BlockSpec((1, tm, w), lambda b, i: (b, i, 0))
        mod = pl.BlockSpec((1, 1, d), lambda b, i: (b, 0, 0))
        oshape = lambda w, dt: jax.ShapeDtypeStruct((b_, t_, w), dt)
        h2_spec = pl.BlockSpec((1, tm * ROW_TILES, LANES), lambda b, i: (b, i, 0))
        h2_shape = jax.ShapeDtypeStruct((b_, t_ * ROW_TILES, LANES), F32)
    else:
        n, d = x.shape
        tm = n
        grid = (1, 1)
        tok = lambda w: pl.BlockSpec((n, w), lambda b, i: (0, 0))
        mod = tok(d)
        oshape = lambda w, dt: jax.ShapeDtypeStruct((n, w), dt)
        h2_spec = pl.BlockSpec((n * ROW_TILES, LANES), lambda b, i: (0, 0))
        h2_shape = jax.ShapeDtypeStruct((n * ROW_TILES, LANES), F32)
    const2 = lambda b, i: (0, 0)
    return pl.pallas_call(
        functools.partial(_outproj_kernel, batched=batched),
        grid=grid,
        in_specs=[
            tok(d), tok(CONV_W), tok(ATT_W), tok(CHUNK_W), mod, mod, mod,
            pl.BlockSpec((d, d), const2),
            pl.BlockSpec((1, d), const2),
            pl.BlockSpec((1, d), const2),
            pl.BlockSpec((d, LANES), const2),
            pl.BlockSpec((1, LANES), const2),
            pl.BlockSpec((SUBLANES, LANES), const2),
        ],
        out_specs=[tok(d), h2_spec, tok(LANES), tok(LANES), pl.BlockSpec((SUBLANES, LANES), const2)],
        out_shape=[oshape(d, F32), h2_shape, oshape(LANES, jnp.int32), oshape(LANES, F32),
                   jax.ShapeDtypeStruct((SUBLANES, LANES), F32)],
        scratch_shapes=[pltpu.VMEM((SUBLANES, LANES), F32)],
        compiler_params=_cparams(("arbitrary", "arbitrary")),
        name="outproj_prompt" if batched else "outproj_sample",
    )(x, co, att, cho, g1, sc2, sh2, w_out_b, ln_g, ln_b, wr_pad, br_pad, carry_in)


def _dispatch_kernel(dest_ref, h_ref, xs_in_ref, xs_ref, sem):
    del xs_in_ref
    td = h_ref.shape[0] // ROW_TILES

    def row_copy(r, dst):
        return pltpu.make_async_copy(h_ref.at[pl.ds(pl.multiple_of(r * ROW_TILES, ROW_TILES), ROW_TILES)],
                                     xs_ref.at[pl.ds(pl.multiple_of(dst * ROW_TILES, ROW_TILES), ROW_TILES)], sem)

    def issue(r, carry):
        for k in range(TOP_K):
            row_copy(r, dest_ref[0, 0, r * TOP_K + k]).start()
        return carry

    lax.fori_loop(0, td, issue, 0, unroll=4)

    def drain(i, carry):
        row_copy(0, 0).wait()
        return carry

    lax.fori_loop(0, td * TOP_K, drain, 0, unroll=8)


def _dispatch(dest, h3d, xs):
    n = h3d.shape[0] // ROW_TILES
    td = min(TD_ROWS, n)
    nt = n // td
    return pl.pallas_call(
        _dispatch_kernel,
        grid=(nt,),
        in_specs=[
            pl.BlockSpec((1, 1, td * TOP_K), lambda i: (i, 0, 0), memory_space=pltpu.SMEM),
            pl.BlockSpec((td * ROW_TILES, LANES), lambda i: (i, 0)),
            pl.BlockSpec(memory_space=pl.ANY),
        ],
        out_specs=pl.BlockSpec(memory_space=pl.ANY),
        scratch_shapes=[pltpu.SemaphoreType.DMA(())],
        out_shape=jax.ShapeDtypeStruct(xs.shape, xs.dtype),
        input_output_aliases={2: 0},
        compiler_params=_cparams(("arbitrary",)),
        name="moe_dispatch",
    )(dest.reshape(nt, 1, td * TOP_K), h3d, xs)


def _expert_kernel(l_ref, be_ref, nu_ref, x_ref, wg_ref, bg_ref, wu_ref, bu_ref, wd_ref, bd_ref,
                   y_ref, wgb_ref, wub_ref, wdb_ref):
    b = pl.program_id(0)
    changed = (b == 0) | (be_ref[b] != be_ref[jnp.maximum(b - 1, 0)])

    @pl.when(changed)
    def _():
        wgb_ref[...] = wg_ref[0, 0].astype(BF16)
        wub_ref[...] = wu_ref[0, 0].astype(BF16)
        wdb_ref[...] = wd_ref[0, 0].astype(BF16)

    @pl.when(b < nu_ref[0])
    def _():
        x = _load_rows_from_tiles(x_ref).astype(BF16)
        g = jnp.minimum(_dot(x, wgb_ref[...]) + bg_ref[0, 0], SWIGLU_LIMIT)
        u = jnp.clip(_dot(x, wub_ref[...]) + bu_ref[0, 0], -SWIGLU_LIMIT, SWIGLU_LIMIT)
        hmid = (u + 1.0) * (g * jax.nn.sigmoid(g * SWIGLU_ALPHA))
        _store_rows_as_tiles(y_ref, _dot(hmid.astype(BF16), wdb_ref[...]) + bd_ref[0, 0])

    @pl.when(b >= nu_ref[0])
    def _():
        y_ref[...] = jnp.zeros(y_ref.shape, y_ref.dtype)


def _experts(layer, block_e, n_used, xs, w_gate, b_gate, w_up, b_up, w_down, b_down):
    n_rows = xs.shape[0] // ROW_TILES
    d = w_gate.shape[-2]
    bm = BM_EXPERT
    nb = n_rows // bm
    dff = w_gate.shape[-1]
    wmap = lambda b, l, be, nu: (l[0], be[b], 0, 0)
    grid_spec = pltpu.PrefetchScalarGridSpec(
        num_scalar_prefetch=3,
        grid=(nb,),
        in_specs=[
            pl.BlockSpec((bm * ROW_TILES, LANES), lambda b, l, be, nu: (b, 0)),
            pl.BlockSpec((1, 1, d, dff), wmap),
            pl.BlockSpec((1, 1, 1, dff), wmap),
            pl.BlockSpec((1, 1, d, dff), wmap),
            pl.BlockSpec((1, 1, 1, dff), wmap),
            pl.BlockSpec((1, 1, dff, d), wmap),
            pl.BlockSpec((1, 1, 1, d), wmap),
        ],
        out_specs=pl.BlockSpec((bm * ROW_TILES, LANES), lambda b, l, be, nu: (b, 0)),
        scratch_shapes=[pltpu.VMEM((d, dff), BF16), pltpu.VMEM((d, dff), BF16), pltpu.VMEM((dff, d), BF16)],
    )
    depth, ne = b_gate.shape[:2]
    return pl.pallas_call(
        _expert_kernel,
        grid_spec=grid_spec,
        out_shape=jax.ShapeDtypeStruct((n_rows * ROW_TILES, LANES), F32),
        compiler_params=_cparams(("arbitrary",)),
        name="moe_experts",
    )(layer, block_e, n_used, xs, w_gate, b_gate.reshape(depth, ne, 1, dff), w_up,
      b_up.reshape(depth, ne, 1, dff), w_down, b_down.reshape(depth, ne, 1, d))


def _combine_kernel(dest_ref, gate_ref, x1_ref, g2_ref, lg_ref, lb_ref, yb_ref, o_ref, buf_ref, sem):
    tc = x1_ref.shape[0]

    def row_copy(src, k, r):
        return pltpu.make_async_copy(yb_ref.at[pl.ds(pl.multiple_of(src * ROW_TILES, ROW_TILES), ROW_TILES)],
                                     buf_ref.at[k, pl.ds(pl.multiple_of(r * ROW_TILES, ROW_TILES), ROW_TILES)], sem)

    def issue(r, carry):
        for k in range(TOP_K):
            row_copy(dest_ref[0, 0, r * TOP_K + k], k, r).start()
        return carry

    lax.fori_loop(0, tc, issue, 0, unroll=4)

    def drain(i, carry):
        row_copy(0, 0, 0).wait()
        return carry

    lax.fori_loop(0, tc * TOP_K, drain, 0, unroll=8)

    gates = gate_ref[...]
    f = gates[:, 0:1] * _load_rows_from_tiles(buf_ref.at[0])
    for k in range(1, TOP_K):
        f = f + gates[:, k:k + 1] * _load_rows_from_tiles(buf_ref.at[k])
    o_ref[...] = _layer_norm_rows(DN_ALPHA * x1_ref[...] + (1.0 + g2_ref[...]) * f,
                                  lg_ref[...], lb_ref[...])


def _combine(dest, gates, x1_2d, g2, ln_g, ln_b, yb, rows_per_mod):
    n, d = x1_2d.shape
    tc = min(TD_ROWS, n)
    nt = n // tc
    if rows_per_mod == 1:
        g2_spec = pl.BlockSpec((tc, d), lambda i: (i, 0))
        g2_in = g2
    else:
        tiles_per_mod = rows_per_mod // tc
        g2_spec = pl.BlockSpec((1, 1, d), lambda i: (i // tiles_per_mod, 0, 0))
        g2_in = g2.reshape(g2.shape[0], 1, d)
    kern = _combine_kernel if rows_per_mod == 1 else _combine_kernel_bcast
    dest_t = dest.reshape(nt, 1, tc * TOP_K)
    return pl.pallas_call(
        kern,
        grid=(nt,),
        in_specs=[
            pl.BlockSpec((1, 1, tc * TOP_K), lambda i: (i, 0, 0), memory_space=pltpu.SMEM),
            pl.BlockSpec((tc, LANES), lambda i: (i, 0)),
            pl.BlockSpec((tc, d), lambda i: (i, 0)),
            g2_spec,
            pl.BlockSpec((1, d), lambda i: (0, 0)),
            pl.BlockSpec((1, d), lambda i: (0, 0)),
            pl.BlockSpec(memory_space=pl.ANY),
        ],
        out_specs=pl.BlockSpec((tc, d), lambda i: (i, 0)),
        scratch_shapes=[pltpu.VMEM((TOP_K, tc * ROW_TILES, LANES), F32), pltpu.SemaphoreType.DMA(())],
        out_shape=jax.ShapeDtypeStruct((n, d), F32),
        compiler_params=_cparams(("arbitrary",)),
        name="moe_combine",
    )(dest_t, gates, x1_2d, g2_in, ln_g, ln_b, yb)


def _combine_kernel_bcast(dest_ref, gate_ref, x1_ref, g2_ref, lg_ref, lb_ref, yb_ref, o_ref, buf_ref, sem):
    _combine_kernel(dest_ref, gate_ref, x1_ref, g2_ref.at[0], lg_ref, lb_ref, yb_ref, o_ref, buf_ref, sem)


def _rope_tables(pos):
    half = QK_DIM // 2
    inv = ROPE_THETA ** (-jnp.arange(half, dtype=F32) / half)
    ang = pos.astype(F32)[:, None] * inv[None, :]
    cos = jnp.cos(ang)
    sin = jnp.sin(ang)
    reps = LANES // QK_DIM
    cos_t = jnp.tile(jnp.concatenate([cos, cos], axis=1), (1, reps))
    sin_t = jnp.tile(jnp.concatenate([-sin, sin], axis=1), (1, reps))
    return cos_t, sin_t


def _pad_rows(a, rows):
    return jnp.concatenate([a, jnp.zeros((rows - a.shape[0],) + a.shape[1:], a.dtype)], axis=0)


def kernel(x_prompt, x_sample, c_prompt, c_sample, cache_k, cache_v, state_conv, page_table, w_ada, b_ada, w_in, conv_w, lambda_q1, lambda_k1, lambda_q2, lambda_k2, subln_g, chunk_ln_g, chunk_ln_b, chunk_ws, chunk_bs, w_out, ln1_g, ln1_b, w_router, b_router, w_gate, b_gate, w_up, b_up, w_down, b_down, ln2_g, ln2_b):
    n_prompt, seq, d = x_prompt.shape
    n_dec, dec_seq, _ = x_sample.shape
    depth = w_in.shape[0]
    n_pool = cache_k.shape[1]
    n_pages = page_table.shape[1]
    past_len = n_pages * PAGE_SIZE
    ns = n_dec * dec_seq
    np_tok = n_prompt * seq
    assert dec_seq >= CONV_K - 1 and dec_seq <= SUBLANES and CHUNK % dec_seq == 0
    assert ns % CHUNK == 0 and seq % CHUNK == 0

    c_all = jnp.concatenate([c_prompt, c_sample], axis=0)
    m_rows = -(-c_all.shape[0] // SUBLANES) * SUBLANES
    mod_all = _ada_call(_pad_rows(c_all, m_rows), w_ada, b_ada)

    cos_p, sin_p = _rope_tables(jnp.arange(seq))
    cos_s, sin_s = _rope_tables(jnp.tile(past_len + jnp.arange(dec_seq), n_dec))

    cache_k4 = jnp.transpose(cache_k, (0, 1, 3, 4, 5, 2)).reshape(depth, n_pool, ATT_HEADS * 2 * QK_DIM, PAGE_SIZE)
    cache_v4 = jnp.transpose(cache_v, (0, 1, 3, 4, 2)).reshape(depth, n_pool, ATT_W, PAGE_SIZE)
    conv0 = jnp.zeros((n_prompt, CONV_K - 1, CONV_W), F32)
    lane = jnp.arange(LANES)
    tok_pos = jnp.arange(ns) % dec_seq

    nrow = 2 * dec_seq * ATT_HEADS
    r_map = jnp.arange(nrow) // (dec_seq * ATT_HEADS)
    r_q = (jnp.arange(nrow) // ATT_HEADS) % dec_seq
    r_h = jnp.arange(nrow) % ATT_HEADS
    col = jnp.arange(ATT_HEADS * 2 * QK_DIM)
    qbd_mask = ((col[None, :] // (2 * QK_DIM)) == r_h[:, None]) & (((col[None, :] // QK_DIM) % 2) == r_map[:, None])

    n_assign = (np_tok + ns) * TOP_K
    n_blocks = -(-n_assign // BM_EXPERT) + N_EXPERTS
    n_rows = n_blocks * BM_EXPERT

    y_p, y_s = x_prompt, x_sample.reshape(ns, d)
    xs = jnp.zeros((n_rows * ROW_TILES, LANES), F32)
    outs = {k: [] for k in ("kp", "vp", "cp", "chp", "ks", "vs", "cs", "chs")}
    for l in range(depth):
        layer = jnp.full((1,), l, jnp.int32)
        lam_init = 0.8 - 0.6 * math.exp(-0.3 * l)
        mod = mod_all[l]
        mp = mod[:n_prompt].reshape(n_prompt, 6, 1, d)
        ms = jnp.repeat(mod[n_prompt:n_prompt + n_dec], dec_seq, axis=0).reshape(ns, 6, d)
        w_in_b = w_in[l].astype(BF16)
        w_out_b = w_out[l].astype(BF16)
        cw = _pad_rows(conv_w[l], SUBLANES)
        lamp = jnp.zeros((SUBLANES, LANES), F32)
        lamp = lamp.at[0, :QK_DIM].set(lambda_q1[l]).at[1, :QK_DIM].set(lambda_k1[l])
        lamp = lamp.at[2, :QK_DIM].set(lambda_q2[l]).at[3, :QK_DIM].set(lambda_k2[l])
        lamp = lamp.at[4, :].set(lam_init)
        lng = chunk_ln_g[l].reshape(1, CHUNK_W)
        lnb = chunk_ln_b[l].reshape(1, CHUNK_W)
        ws = chunk_ws[l]
        bs = chunk_bs[l]
        wcat_p = jnp.concatenate([jnp.tril(ws[g]) for g in range(CHUNK_GROUPS)], axis=1).astype(BF16)
        bsf_p = jnp.repeat(bs.T, CHUNK_GW, axis=1)
        eye = jnp.eye(CHUNK // dec_seq, dtype=F32)
        wcat_s = jnp.concatenate([jnp.kron(eye, jnp.tril(ws[g, :dec_seq, :dec_seq]))
                                  for g in range(CHUNK_GROUPS)], axis=1).astype(BF16)
        bsf_s = jnp.tile(jnp.repeat(bs[:, :dec_seq].T, CHUNK_GW, axis=1), (CHUNK // dec_seq, 1))
        g_sub = subln_g[l]
        wr_pad = jnp.concatenate([w_router[l], jnp.zeros((d, LANES - N_EXPERTS), F32)], axis=1)
        br_pad = jnp.concatenate([b_router[l], jnp.full((LANES - N_EXPERTS,), NEG_BIG, F32)]).reshape(1, LANES)
        l1g, l1b = ln1_g[l].reshape(1, d), ln1_b[l].reshape(1, d)
        l2g, l2b = ln2_g[l].reshape(1, d), ln2_b[l].reshape(1, d)

        (q_p, kf_p, kb_p, vf_p, vb_p, co_p, cho_p, cst_p, chst_p) = _inproj_prompt(
            y_p, mp[:, 1], mp[:, 0], w_in_b, cos_p, sin_p, cw, conv0, lng, lnb, wcat_p, bsf_p)
        att_p = _attn_prompt(q_p, kb_p, vb_p, lamp, jnp.broadcast_to(g_sub[:, None], (V_DIM, min(TQ_ATT, seq))))
        zero_carry = jnp.zeros((SUBLANES, LANES), F32)
        x1_p, h2_p, route_p, gate_p, cnt_p = _outproj(
            y_p, co_p, att_p, cho_p, mp[:, 2], mp[:, 4], mp[:, 3], w_out_b, l1g, l1b, wr_pad, br_pad,
            zero_carry, True)

        st = state_conv[l]
        prev1 = jnp.repeat(st[:, 1], dec_seq, axis=0)
        prev2 = jnp.where((tok_pos == 0)[:, None], jnp.repeat(st[:, 0], dec_seq, axis=0), prev1)
        (q_s, kf_s, kb_s, vf_s, vb_s, co_s, cho_s, z_s, vn_s) = _inproj_sample(
            y_s, ms[:, 1], ms[:, 0], w_in_b, cos_s, sin_s, cw, prev1, prev2, lng, lnb, wcat_s, bsf_s, dec_seq)
        q3 = q_s.reshape(n_dec, dec_seq, -1)
        qbd = jnp.where(qbd_mask[None], q3[:, r_q, :], jnp.zeros((), BF16))
        pad3 = lambda a: jnp.concatenate(
            [a.reshape(n_dec, dec_seq, -1),
             jnp.zeros((n_dec, NEW_PAD - dec_seq, a.shape[-1]), a.dtype)], axis=1)
        att_s = _attn_sample(layer, page_table, cache_k4, cache_v4, qbd, pad3(kf_s), pad3(vf_s), lamp,
                             jnp.tile(g_sub, ATT_HEADS).reshape(1, ATT_W), dec_seq)
        x1_s, h2_s, route_s, gate_s, cnt_all = _outproj(
            y_s, co_s, att_s.reshape(ns, ATT_W).astype(BF16), cho_s, ms[:, 2], ms[:, 4], ms[:, 3],
            w_out_b, l1g, l1b, wr_pad, br_pad, cnt_p, False)

        counts = cnt_all[0, :N_EXPERTS].astype(jnp.int32)
        pcounts = ((counts + BM_EXPERT - 1) // BM_EXPERT) * BM_EXPERT
        pends = jnp.cumsum(pcounts)
        pstart = (pends - pcounts).astype(jnp.int32)
        blk_row = jnp.arange(n_blocks, dtype=jnp.int32) * BM_EXPERT
        block_e = jnp.minimum(jnp.sum(blk_row[:, None] >= pends[None, :], axis=1), N_EXPERTS - 1).astype(jnp.int32)
        n_used = (pends[-1] // BM_EXPERT).astype(jnp.int32).reshape(1)
        def dest_rows(route):
            idx, rank = route[:, :TOP_K], route[:, TOP_K:2 * TOP_K]
            hot = idx[:, :, None] == jnp.arange(N_EXPERTS, dtype=jnp.int32)[None, None, :]
            return rank + jnp.sum(jnp.where(hot, pstart[None, None, :], 0), axis=-1)

        dest_p = dest_rows(route_p.reshape(np_tok, LANES))
        dest_s = dest_rows(route_s)
        xs = _dispatch(dest_p, h2_p.reshape(np_tok * ROW_TILES, LANES), xs)
        xs = _dispatch(dest_s, h2_s, xs)
        yb = _experts(layer, block_e, n_used, xs, w_gate, b_gate, w_up, b_up, w_down, b_down)
        y_p = _combine(dest_p, gate_p.reshape(np_tok, LANES), x1_p.reshape(np_tok, d),
                       mod[:n_prompt, 5 * d:], l2g, l2b, yb, seq).reshape(n_prompt, seq, d)
        y_s = _combine(dest_s, gate_s, x1_s, ms[:, 5], l2g, l2b, yb, 1)

        outs["kp"].append(jnp.transpose(kf_p.reshape(n_prompt, ATT_HEADS, 2, QK_DIM, seq), (0, 4, 1, 2, 3)))
        outs["vp"].append(jnp.transpose(vf_p.reshape(n_prompt, ATT_HEADS, V_DIM, seq), (0, 3, 1, 2)))
        outs["cp"].append(cst_p)
        outs["chp"].append(chst_p)
        outs["ks"].append(kf_s.reshape(n_dec, dec_seq, ATT_HEADS, 2, QK_DIM))
        outs["vs"].append(vf_s.reshape(n_dec, dec_seq, ATT_HEADS, V_DIM))
        outs["cs"].append(z_s.reshape(n_dec, dec_seq, CONV_W)[:, dec_seq - (CONV_K - 1):])
        outs["chs"].append(vn_s.reshape(n_dec, dec_seq, CHUNK_W))

    st_ = lambda k: jnp.stack(outs[k])
    return (y_p, y_s.reshape(n_dec, dec_seq, d), st_("kp"), st_("vp"), st_("cp"), st_("chp"),
            st_("ks"), st_("vs"), st_("cs"), st_("chs"))
```

```python
import functools
import math

import jax
import jax.numpy as jnp
from jax import lax
from jax.experimental import pallas as pl
from jax.experimental.pallas import tpu as pltpu

F32 = jnp.float32
BF16 = jnp.bfloat16

D_MODEL = 1024
CONV_W = 256
CONV_K = 3
ATT_HEADS = 8
QK_DIM = 32
V_DIM = 64
ATT_W = ATT_HEADS * V_DIM
CHUNK_GROUPS = 4
CHUNK_GW = 64
CHUNK_W = CHUNK_GROUPS * CHUNK_GW
CHUNK = 128
O_CH = 0
O_Q = 3 * CONV_W
O_K = O_Q + ATT_HEADS * 2 * QK_DIM
O_V = O_K + ATT_HEADS * 2 * QK_DIM
O_U = O_V + ATT_W
O_SV = O_U + CHUNK_W
IN_W = O_SV + CHUNK_W
N_EXPERTS = 32
TOP_K = 4
D_FF = 1024
SWIGLU_ALPHA = 1.702
SWIGLU_LIMIT = 7.0
PAGE_SIZE = 128
ROPE_THETA = 10000.0
LN_EPS = 1e-5
DEPTH = 4
DN_ALPHA = (2 * DEPTH) ** 0.25
Q_SCALE = QK_DIM ** -0.5 * math.log2(math.e)

LANES = 128
SUBLANES = 8
ROW_TILES = D_MODEL // LANES
VMEM_BYTES_V7X = 64 * 1024 * 1024

TM_PROJ = 512
TQ_ATT = 512
TK_ATT = 512
PAGES_PER_STEP = 8
BM_EXPERT = 256
TD_ROWS = 256
NEW_PAD = PAGE_SIZE
NEG_BIG = -1e30
VMEM_LIMIT = 56 * 1024 * 1024


def _cparams(sem):
    return pltpu.CompilerParams(dimension_semantics=sem, vmem_limit_bytes=VMEM_LIMIT)


def _split_bf16(a):
    hi = a.astype(BF16)
    lo = (a - hi.astype(F32)).astype(BF16)
    return hi, lo


def _dot(a, b):
    return jnp.dot(a, b, preferred_element_type=F32)


def _dot_split_lhs(a, b_bf16):
    hi, lo = _split_bf16(a)
    return _dot(hi, b_bf16) + _dot(lo, b_bf16)


def _dot3(a, b):
    ah, al = _split_bf16(a)
    bh, bl = _split_bf16(b)
    return _dot(ah, bh) + (_dot(al, bh) + _dot(ah, bl))


def _store_rows_as_tiles(ref, val):
    rows = val.shape[0]
    for j in range(ROW_TILES):
        ref[pl.ds(j, rows, stride=ROW_TILES), :] = val[:, j * LANES:(j + 1) * LANES]


def _load_rows_from_tiles(ref):
    rows = ref.shape[0] // ROW_TILES
    return jnp.concatenate([ref[pl.ds(j, rows, stride=ROW_TILES), :] for j in range(ROW_TILES)], axis=1)


def _layer_norm_rows(v, g, b):
    mu = jnp.mean(v, axis=-1, keepdims=True)
    d = v - mu
    var = jnp.mean(d * d, axis=-1, keepdims=True)
    return d * lax.rsqrt(var + LN_EPS) * g + b


def _ada_kernel(c_ref, w_ref, b_ref, o_ref):
    o_ref[0] = _dot3(c_ref[...], w_ref[0]) + b_ref[0]


def _ada_call(c_all, w_ada, b_ada):
    depth, d, n6 = w_ada.shape
    m = c_all.shape[0]
    tn = 1024
    return pl.pallas_call(
        _ada_kernel,
        grid=(depth, n6 // tn),
        in_specs=[
            pl.BlockSpec((m, d), lambda l, j: (0, 0)),
            pl.BlockSpec((1, d, tn), lambda l, j: (l, 0, j)),
            pl.BlockSpec((1, 1, tn), lambda l, j: (l, 0, j)),
        ],
        out_specs=pl.BlockSpec((1, m, tn), lambda l, j: (l, 0, j)),
        out_shape=jax.ShapeDtypeStruct((depth, m, n6), F32),
        compiler_params=_cparams(("arbitrary", "arbitrary")),
        name="ada",
    )(c_all, w_ada, b_ada.reshape(depth, 1, n6))


def _rope_chunks(p, cos_t, sin_t, scale):
    lane = lax.broadcasted_iota(jnp.int32, (1, LANES), 1)
    first_half = (lane % QK_DIM) < (QK_DIM // 2)
    outs = []
    for c in range(p.shape[1] // LANES):
        xc = p[:, c * LANES:(c + 1) * LANES]
        up = pltpu.roll(xc, LANES - QK_DIM // 2, 1)
        dn = pltpu.roll(xc, QK_DIM // 2, 1)
        r = xc * cos_t + jnp.where(first_half, up, dn) * sin_t
        if scale is not None:
            r = r * scale
        outs.append(r)
    return jnp.concatenate(outs, axis=1)


def _group_norm(sv, g, b):
    ri = lax.broadcasted_iota(jnp.int32, (CHUNK_W, CHUNK_W), 0) // CHUNK_GW
    ci = lax.broadcasted_iota(jnp.int32, (CHUNK_W, CHUNK_W), 1) // CHUNK_GW
    avg = jnp.where(ri == ci, 1.0 / CHUNK_GW, 0.0).astype(BF16)
    mu = _dot_split_lhs(sv, avg)
    d = sv - mu
    var = _dot_split_lhs(d * d, avg)
    return d * lax.rsqrt(var + LN_EPS) * g + b


def _chunk_mix(vn, wcat_ref, bsf_ref):
    gid = lax.broadcasted_iota(jnp.int32, (1, CHUNK_W), 1) // CHUNK_GW
    vb = vn.astype(BF16)
    zero = jnp.zeros_like(vb[:CHUNK])
    outs = []
    for c in range(vn.shape[0] // CHUNK):
        vc = vb[c * CHUNK:(c + 1) * CHUNK]
        stack = jnp.concatenate([jnp.where(gid == g, vc, zero) for g in range(CHUNK_GROUPS)], axis=0)
        outs.append(_dot(wcat_ref[...], stack) + bsf_ref[...])
    return jnp.concatenate(outs, axis=0)


def _inproj_kernel(*refs, seq_mode, dec_seq):
    if seq_mode:
        (x_ref, sc_ref, sh_ref, w_ref, cos_ref, sin_ref, cw_ref, cprev_ref, lng_ref, lnb_ref,
         wcat_ref, bsf_ref,
         q_ref, kf_ref, kb_ref, vf_ref, vb_ref, co_ref, cho_ref, cst_ref, chst_ref, zc_ref) = refs
        x = x_ref[0]
        sc = sc_ref[0]
        sh = sh_ref[0]
    else:
        (x_ref, sc_ref, sh_ref, w_ref, cos_ref, sin_ref, cw_ref, p1_ref, p2_ref, lng_ref, lnb_ref,
         wcat_ref, bsf_ref,
         q_ref, kf_ref, kb_ref, vf_ref, vb_ref, co_ref, cho_ref, cst_ref, chst_ref) = refs
        x = x_ref[...]
        sc = sc_ref[...]
        sh = sh_ref[...]
    tm = x.shape[0]
    h = (x * (1.0 + sc) + sh).astype(BF16)

    pc = _dot(h, w_ref[:, O_CH:O_Q])
    conv_in, gate_b, gate_c = pc[:, :CONV_W], pc[:, CONV_W:2 * CONV_W], pc[:, 2 * CONV_W:]
    z = gate_c * conv_in
    row = lax.broadcasted_iota(jnp.int32, (tm, 1), 0)
    z1 = pltpu.roll(z, 1, 0)
    z2 = pltpu.roll(z, 2, 0)
    if seq_mode:
        @pl.when(pl.program_id(1) == 0)
        def _():
            zc_ref[0:2, :] = cprev_ref[0]
        c0 = zc_ref[0:1, :]
        c1 = zc_ref[1:2, :]
        z1 = jnp.where(row == 0, c1, z1)
        z2 = jnp.where(row == 0, c0, jnp.where(row == 1, c1, z2))
        zc_ref[0:2, :] = z[tm - 2:tm, :]
        cst_ref[0] = z[tm - 2:tm, :]
    else:
        pos_in_seq = row % dec_seq
        z1 = jnp.where(pos_in_seq >= 1, z1, p1_ref[...])
        z2 = jnp.where(pos_in_seq >= 2, z2, p2_ref[...])
        cst_ref[...] = z
    y = cw_ref[0:1, :] * z2 + cw_ref[1:2, :] * z1 + cw_ref[2:3, :] * z
    conv_out = (gate_b * y).astype(BF16)

    pqk = _dot(h, w_ref[:, O_Q:O_V])
    cos_t = cos_ref[...]
    sin_t = sin_ref[...]
    nq = O_K - O_Q
    qr = _rope_chunks(pqk[:, :nq], cos_t, sin_t, Q_SCALE)
    kr = _rope_chunks(pqk[:, nq:], cos_t, sin_t, None)
    pv = _dot(h, w_ref[:, O_V:O_U])

    pu = _dot(h, w_ref[:, O_U:IN_W])
    u, sv = pu[:, :CHUNK_W], pu[:, CHUNK_W:]
    vn = _group_norm(sv, lng_ref[...], lnb_ref[...])
    chunk_out = (u * _chunk_mix(vn, wcat_ref, bsf_ref)).astype(BF16)

    if seq_mode:
        q_ref[0] = qr.T.astype(BF16)
        kf_ref[0] = kr.T
        kb_ref[0] = kr.astype(BF16)
        vt = pv.T
        vf_ref[0] = vt
        vtb = vt.astype(BF16)
        for c in range(tm // TK_ATT):
            vb_ref[0, c] = vtb[:, c * TK_ATT:(c + 1) * TK_ATT]
        co_ref[0] = conv_out
        cho_ref[0] = chunk_out
        chst_ref[0] = vn[tm - CHUNK:tm, :]
    else:
        q_ref[...] = qr.astype(BF16)
        kf_ref[...] = kr
        kb_ref[...] = kr.astype(BF16)
        vf_ref[...] = pv
        vb_ref[...] = pv.astype(BF16)
        co_ref[...] = conv_out
        cho_ref[...] = chunk_out
        chst_ref[...] = vn


def _inproj_prompt(x, sc1, sh1, w_in_b, cos_t, sin_t, conv_w, conv_prev, ln_g, ln_b, wcat, bsf):
    b_, t_, d = x.shape
    tm = min(TM_PROJ, t_)
    nt = t_ // tm
    const2 = lambda b, i: (0, 0)
    tok3 = lambda b, i: (b, i, 0)
    tokT = lambda b, i: (b, 0, i)
    bat3 = lambda b, i: (b, 0, 0)
    qk_w = ATT_HEADS * 2 * QK_DIM
    outs = pl.pallas_call(
        functools.partial(_inproj_kernel, seq_mode=True, dec_seq=0),
        grid=(b_, nt),
        in_specs=[
            pl.BlockSpec((1, tm, d), tok3),
            pl.BlockSpec((1, 1, d), bat3),
            pl.BlockSpec((1, 1, d), bat3),
            pl.BlockSpec((d, IN_W), const2),
            pl.BlockSpec((tm, LANES), lambda b, i: (i, 0)),
            pl.BlockSpec((tm, LANES), lambda b, i: (i, 0)),
            pl.BlockSpec((SUBLANES, CONV_W), const2),
            pl.BlockSpec((1, CONV_K - 1, CONV_W), bat3),
            pl.BlockSpec((1, CHUNK_W), const2),
            pl.BlockSpec((1, CHUNK_W), const2),
            pl.BlockSpec((CHUNK, CHUNK_GROUPS * CHUNK), const2),
            pl.BlockSpec((CHUNK, CHUNK_W), const2),
        ],
        out_specs=[
            pl.BlockSpec((1, qk_w, tm), tokT),
            pl.BlockSpec((1, qk_w, tm), tokT),
            pl.BlockSpec((1, tm, qk_w), tok3),
            pl.BlockSpec((1, ATT_W, tm), tokT),
            pl.BlockSpec((1, tm // TK_ATT, ATT_W, TK_ATT), lambda b, i: (b, i, 0, 0)),
            pl.BlockSpec((1, tm, CONV_W), tok3),
            pl.BlockSpec((1, tm, CHUNK_W), tok3),
            pl.BlockSpec((1, CONV_K - 1, CONV_W), bat3),
            pl.BlockSpec((1, CHUNK, CHUNK_W), bat3),
        ],
        out_shape=[
            jax.ShapeDtypeStruct((b_, qk_w, t_), BF16),
            jax.ShapeDtypeStruct((b_, qk_w, t_), F32),
            jax.ShapeDtypeStruct((b_, t_, qk_w), BF16),
            jax.ShapeDtypeStruct((b_, ATT_W, t_), F32),
            jax.ShapeDtypeStruct((b_, t_ // TK_ATT, ATT_W, TK_ATT), BF16),
            jax.ShapeDtypeStruct((b_, t_, CONV_W), BF16),
            jax.ShapeDtypeStruct((b_, t_, CHUNK_W), BF16),
            jax.ShapeDtypeStruct((b_, CONV_K - 1, CONV_W), F32),
            jax.ShapeDtypeStruct((b_, CHUNK, CHUNK_W), F32),
        ],
        scratch_shapes=[pltpu.VMEM((SUBLANES, CONV_W), F32)],
        compiler_params=_cparams(("arbitrary", "arbitrary")),
        name="inproj_prompt",
    )(x, sc1, sh1, w_in_b, cos_t, sin_t, conv_w, conv_prev, ln_g, ln_b, wcat, bsf)
    return outs


def _inproj_sample(x2d, sc1, sh1, w_in_b, cos_t, sin_t, conv_w, prev1, prev2, ln_g, ln_b, wcat, bsf,
                   dec_seq):
    n, d = x2d.shape
    qk_w = ATT_HEADS * 2 * QK_DIM
    full = lambda shape: pl.BlockSpec(shape, lambda i: tuple(0 for _ in shape))
    outs = pl.pallas_call(
        functools.partial(_inproj_kernel, seq_mode=False, dec_seq=dec_seq),
        grid=(1,),
        in_specs=[
            full((n, d)), full((n, d)), full((n, d)), full((d, IN_W)),
            full((n, LANES)), full((n, LANES)), full((SUBLANES, CONV_W)),
            full((n, CONV_W)), full((n, CONV_W)), full((1, CHUNK_W)), full((1, CHUNK_W)),
            full((CHUNK, CHUNK_GROUPS * CHUNK)), full((CHUNK, CHUNK_W)),
        ],
        out_specs=[
            full((n, qk_w)), full((n, qk_w)), full((n, qk_w)), full((n, ATT_W)), full((n, ATT_W)),
            full((n, CONV_W)), full((n, CHUNK_W)), full((n, CONV_W)), full((n, CHUNK_W)),
        ],
        out_shape=[
            jax.ShapeDtypeStruct((n, qk_w), BF16),
            jax.ShapeDtypeStruct((n, qk_w), F32),
            jax.ShapeDtypeStruct((n, qk_w), BF16),
            jax.ShapeDtypeStruct((n, ATT_W), F32),
            jax.ShapeDtypeStruct((n, ATT_W), BF16),
            jax.ShapeDtypeStruct((n, CONV_W), BF16),
            jax.ShapeDtypeStruct((n, CHUNK_W), BF16),
            jax.ShapeDtypeStruct((n, CONV_W), F32),
            jax.ShapeDtypeStruct((n, CHUNK_W), F32),
        ],
        compiler_params=_cparams(("arbitrary",)),
        name="inproj_sample",
    )(x2d, sc1, sh1, w_in_b, cos_t, sin_t, conv_w, prev1, prev2, ln_g, ln_b, wcat, bsf)
    return outs


def _lambda_from(lam_ref):
    lp = lam_ref[...]
    s1 = jnp.sum(lp[0:1, :] * lp[1:2, :], axis=1, keepdims=True)
    s2 = jnp.sum(lp[2:3, :] * lp[3:4, :], axis=1, keepdims=True)
    lam_init = lp[4:5, 0:1]
    return jnp.exp(s1) - jnp.exp(s2) + lam_init, lam_init


L_ROWS = 16


def _attn_prompt_kernel(qt_ref, k_ref, vt_ref, lam_ref, g_ref, o_ref, qm_ref, m_ref, acc_ref, s0_ref, s1_ref):
    tq = qt_ref.shape[2]
    tk = vt_ref.shape[3]
    qi = pl.program_id(2)
    n_hm = 2 * 2

    qt = qt_ref[0]
    row_hm = lax.broadcasted_iota(jnp.int32, (LANES, 1), 0) // QK_DIM
    zero = jnp.zeros_like(qt)
    for hm in range(n_hm):
        qm_ref[hm] = jnp.where(row_hm == hm, qt, zero)
    m_ref[...] = jnp.full(m_ref.shape, NEG_BIG, F32)
    acc_ref[...] = jnp.zeros(acc_ref.shape, F32)

    s_bufs = (s0_ref, s1_ref)

    def scores(ki, slot):
        start = pl.multiple_of(ki * tk, tk)
        kt = k_ref[0, pl.ds(start, tk), :]
        for hm in range(n_hm):
            s_bufs[slot][hm] = _dot(kt, qm_ref[hm])

    def consume(ki, slot, masked):
        vt = vt_ref[0, ki]
        ones = jnp.ones((L_ROWS, tk), BF16)
        m_old = [m_ref[hm] for hm in range(n_hm)]
        acc_old = [acc_ref[hm] for hm in range(n_hm)]
        lhs = [jnp.concatenate([vt[h * V_DIM:(h + 1) * V_DIM, :], ones], axis=0) for h in range(2)]
        s = [s_bufs[slot][hm] for hm in range(n_hm)]
        if masked:
            key = lax.broadcasted_iota(jnp.int32, (tk, tq), 0)
            qry = lax.broadcasted_iota(jnp.int32, (tk, tq), 1)
            valid = key <= qry
            s = [jnp.where(valid, s_, NEG_BIG) for s_ in s]
        m_new = [jnp.maximum(m_old[hm], jnp.max(s[hm], axis=0, keepdims=True)) for hm in range(n_hm)]
        p = [jnp.exp2(s[hm] - m_new[hm]).astype(BF16) for hm in range(n_hm)]
        pv = [_dot(lhs[hm // 2], p[hm]) for hm in range(n_hm)]
        for hm in range(n_hm):
            acc_ref[hm] = jnp.exp2(m_old[hm] - m_new[hm]) * acc_old[hm] + pv[hm]
            m_ref[hm] = m_new[hm]

    scores(0, 0)

    def pair(j, carry):
        scores(2 * j + 1, 1)
        consume(2 * j, 0, False)
        scores(2 * j + 2, 0)
        consume(2 * j + 1, 1, False)
        return carry

    lax.fori_loop(0, qi // 2, pair, 0)

    @pl.when(qi % 2 == 0)
    def _():
        consume(qi, 0, True)

    @pl.when(qi % 2 == 1)
    def _():
        scores(qi, 1)
        consume(qi - 1, 0, False)
        consume(qi, 1, True)

    lam, lam_init = _lambda_from(lam_ref)
    halves = []
    for h in range(2):
        a0 = acc_ref[2 * h]
        a1 = acc_ref[2 * h + 1]
        o0 = a0[:V_DIM] * (1.0 / a0[V_DIM:V_DIM + 1])
        o1 = a1[:V_DIM] * (1.0 / a1[V_DIM:V_DIM + 1])
        d = o0 - lam * o1
        ms = jnp.sum(d * d, axis=0, keepdims=True) * (1.0 / V_DIM)
        halves.append(d * lax.rsqrt(ms + LN_EPS) * g_ref[...] * (1.0 - lam_init))
    o_ref[0] = jnp.concatenate(halves, axis=0).T.astype(o_ref.dtype)


def _attn_prompt(qt, k, vt4, lamp, g_col):
    b_, t_, _ = k.shape
    tk = vt4.shape[3]
    tq = g_col.shape[1]
    assert tq == tk, "the causal diagonal is handled as one square tile"
    nq = t_ // tq
    npair = ATT_HEADS // 2
    return pl.pallas_call(
        _attn_prompt_kernel,
        grid=(b_, npair, nq),
        in_specs=[
            pl.BlockSpec((1, LANES, tq), lambda b, h, i: (b, h, i)),
            pl.BlockSpec((1, t_, LANES), lambda b, h, i: (b, 0, h)),
            pl.BlockSpec((1, t_ // tk, LANES, tk), lambda b, h, i: (b, 0, h, 0)),
            pl.BlockSpec((SUBLANES, LANES), lambda b, h, i: (0, 0)),
            pl.BlockSpec((V_DIM, tq), lambda b, h, i: (0, 0)),
        ],
        out_specs=pl.BlockSpec((1, tq, LANES), lambda b, h, i: (b, i, h)),
        out_shape=jax.ShapeDtypeStruct((b_, t_, ATT_W), BF16),
        scratch_shapes=[
            pltpu.VMEM((4, LANES, tq), BF16),
            pltpu.VMEM((4, 1, tq), F32),
            pltpu.VMEM((4, V_DIM + L_ROWS, tq), F32),
            pltpu.VMEM((4, tk, tq), F32),
            pltpu.VMEM((4, tk, tq), F32),
        ],
        compiler_params=_cparams(("arbitrary", "arbitrary", "arbitrary")),
        name="attn_prompt",
    )(qt, k, vt4, lamp, g_col)


def _attn_sample_kernel(l_ref, pt_ref, *refs, pps, dec_seq):
    k_refs = refs[:pps]
    v_refs = refs[pps:2 * pps]
    qbd_ref, kn_ref, vn_ref, lam_ref, g_ref, o_ref, m_ref, l_acc_ref, acc_ref = refs[2 * pps:]
    j = pl.program_id(1)
    nrow = qbd_ref.shape[1]
    half = nrow // 2

    @pl.when(j == 0)
    def _():
        m_ref[...] = jnp.full(m_ref.shape, NEG_BIG, F32)
        l_acc_ref[...] = jnp.zeros(l_acc_ref.shape, F32)
        acc_ref[...] = jnp.zeros(acc_ref.shape, F32)

    qbd = qbd_ref[0]

    nt = (((1,), (1,)), ((), ()))

    def online(s, vals, vals_transposed):
        m_old = m_ref[...]
        m_new = jnp.maximum(m_old, jnp.max(s, axis=1, keepdims=True))
        alpha = jnp.exp2(m_old - m_new)
        p = jnp.exp2(s - m_new)
        m_ref[...] = m_new
        l_acc_ref[...] = alpha * l_acc_ref[...] + jnp.sum(p, axis=1, keepdims=True)
        pb = p.astype(BF16)
        pv = None
        for r, val in enumerate(vals):
            pr = pb[:, r * PAGE_SIZE:(r + 1) * PAGE_SIZE]
            if vals_transposed:
                t = lax.dot_general(pr, val, nt, preferred_element_type=F32)
            else:
                t = _dot(pr, val)
            pv = t if pv is None else pv + t
        acc_ref[...] = alpha * acc_ref[...] + pv

    s_parts = [_dot(qbd, k_refs[r][0, 0].astype(BF16)) for r in range(pps)]
    online(jnp.concatenate(s_parts, axis=1), [v_refs[r][0, 0].astype(BF16) for r in range(pps)], True)

    @pl.when(j == pl.num_programs(1) - 1)
    def _():
        kn = kn_ref[0].astype(BF16)
        s = lax.dot_general(qbd, kn, nt, preferred_element_type=F32)
        r_ = lax.broadcasted_iota(jnp.int32, s.shape, 0)
        c_ = lax.broadcasted_iota(jnp.int32, s.shape, 1)
        qpos = (r_ % half) // ATT_HEADS
        s = jnp.where(c_ <= qpos, s, NEG_BIG)
        online(s, [vn_ref[0].astype(BF16)], False)

        lam, lam_init = _lambda_from(lam_ref)
        o = acc_ref[...] / l_acc_ref[...]
        d = o[:half] - lam * o[half:]
        rr = lax.broadcasted_iota(jnp.int32, d.shape, 0) % ATT_HEADS
        cc = lax.broadcasted_iota(jnp.int32, d.shape, 1) // V_DIM
        d = jnp.where(rr == cc, d, 0.0)
        ms = jnp.sum(d * d, axis=1, keepdims=True) * (1.0 / V_DIM)
        dn = d * lax.rsqrt(ms + LN_EPS) * g_ref[...] * (1.0 - lam_init)
        o_ref[0] = jnp.sum(dn.reshape(dec_seq, ATT_HEADS, ATT_W), axis=1)


def _attn_sample(layer, page_table, cache_k4, cache_v4, qbd, kn_pad, vn_pad, lamp, g8, dec_seq):
    n_dec, n_pages = page_table.shape
    pps = PAGES_PER_STEP
    while n_pages % pps:
        pps //= 2
    nrow = qbd.shape[1]
    qkw = qbd.shape[2]

    def page_map(r):
        return lambda b, j, l_ref, pt_ref: (l_ref[0], pt_ref[b * n_pages + j * pps + r], 0, 0)

    seq3 = lambda b, j, l_ref, pt_ref: (b, 0, 0)
    const2 = lambda b, j, l_ref, pt_ref: (0, 0)
    in_specs = ([pl.BlockSpec((1, 1, qkw, PAGE_SIZE), page_map(r)) for r in range(pps)]
                + [pl.BlockSpec((1, 1, ATT_W, PAGE_SIZE), page_map(r)) for r in range(pps)]
                + [pl.BlockSpec((1, nrow, qkw), seq3),
                   pl.BlockSpec((1, NEW_PAD, qkw), seq3),
                   pl.BlockSpec((1, NEW_PAD, ATT_W), seq3),
                   pl.BlockSpec((SUBLANES, LANES), const2),
                   pl.BlockSpec((1, ATT_W), const2)])
    grid_spec = pltpu.PrefetchScalarGridSpec(
        num_scalar_prefetch=2,
        grid=(n_dec, n_pages // pps),
        in_specs=in_specs,
        out_specs=pl.BlockSpec((1, dec_seq, ATT_W), seq3),
        scratch_shapes=[
            pltpu.VMEM((nrow, 1), F32),
            pltpu.VMEM((nrow, 1), F32),
            pltpu.VMEM((nrow, ATT_W), F32),
        ],
    )
    return pl.pallas_call(
        functools.partial(_attn_sample_kernel, pps=pps, dec_seq=dec_seq),
        grid_spec=grid_spec,
        out_shape=jax.ShapeDtypeStruct((n_dec, dec_seq, ATT_W), F32),
        compiler_params=_cparams(("arbitrary", "arbitrary")),
        name="attn_sample",
    )(layer, page_table.reshape(-1), *([cache_k4] * pps), *([cache_v4] * pps),
      qbd, kn_pad, vn_pad, lamp, g8)


def _outproj_kernel(x_ref, co_ref, att_ref, cho_ref, g1_ref, sc2_ref, sh2_ref, wo_ref, lg_ref, lb_ref,
                    wr_ref, br_ref, cin_ref,
                    x1_ref, h2_ref, route_ref, gate_ref, cnt_ref, carry_ref, *, batched):
    first = (pl.program_id(0) == 0) & (pl.program_id(1) == 0)

    @pl.when(first)
    def _():
        carry_ref[...] = cin_ref[...]

    if batched:
        x, co, att, cho = x_ref[0], co_ref[0], att_ref[0], cho_ref[0]
        g1, sc2, sh2 = g1_ref[0], sc2_ref[0], sh2_ref[0]
    else:
        x, co, att, cho = x_ref[...], co_ref[...], att_ref[...], cho_ref[...]
        g1, sc2, sh2 = g1_ref[...], sc2_ref[...], sh2_ref[...]
    tm = x.shape[0]
    mix = jnp.concatenate([co, att, cho], axis=1)
    y = _dot(mix, wo_ref[...])
    x1 = _layer_norm_rows(DN_ALPHA * x + (1.0 + g1) * y, lg_ref[...], lb_ref[...])
    h2 = x1 * (1.0 + sc2) + sh2

    logits = _dot3(h2, wr_ref[...]) + br_ref[...]
    lane = lax.broadcasted_iota(jnp.int32, (1, LANES), 1)
    lane_f = lane.astype(F32)
    work = logits
    vals, hots, idxs = [], [], []
    for _ in range(TOP_K):
        mx = jnp.max(work, axis=1, keepdims=True)
        idx = jnp.min(jnp.where(work == mx, lane_f, float(LANES)), axis=1, keepdims=True)
        hot = lane_f == idx
        work = jnp.where(hot, -3e38, work)
        vals.append(mx)
        hots.append(hot)
        idxs.append(idx)
    exps = [jnp.exp(v - vals[0]) for v in vals]
    inv = 1.0 / (exps[0] + exps[1] + exps[2] + exps[3])

    any_hot = hots[0] | hots[1] | hots[2] | hots[3]
    a_b = jnp.where(any_hot, 1.0, 0.0).astype(BF16)
    ri = lax.broadcasted_iota(jnp.int32, (tm, tm), 0)
    ci = lax.broadcasted_iota(jnp.int32, (tm, tm), 1)
    stril = jnp.where(ci < ri, 1.0, 0.0).astype(BF16)
    before = _dot(stril, a_b) + carry_ref[0:1, :]
    new_carry = carry_ref[0:1, :] + jnp.sum(a_b.astype(F32), axis=0, keepdims=True)
    carry_ref[0:1, :] = new_carry

    route = jnp.zeros((tm, LANES), F32)
    gates = jnp.zeros((tm, LANES), F32)
    for k in range(TOP_K):
        rank_k = jnp.sum(jnp.where(hots[k], before, 0.0), axis=1, keepdims=True)
        route = jnp.where(lane == k, idxs[k], route)
        route = jnp.where(lane == TOP_K + k, rank_k, route)
        gates = jnp.where(lane == k, exps[k] * inv, gates)

    if batched:
        x1_ref[0] = x1
        _store_rows_as_tiles(h2_ref.at[0], h2)
        route_ref[0] = route.astype(jnp.int32)
        gate_ref[0] = gates
    else:
        x1_ref[...] = x1
        _store_rows_as_tiles(h2_ref, h2)
        route_ref[...] = route.astype(jnp.int32)
        gate_ref[...] = gates
    cnt_ref[...] = jnp.broadcast_to(new_carry, cnt_ref.shape)


def _outproj(x, co, att, cho, g1, sc2, sh2, w_out_b, ln_g, ln_b, wr_pad, br_pad, carry_in, batched):
    if batched:
        b_, t_, d = x.shape
        tm = min(TM_PROJ, t_)
        grid = (b_, t_ // tm)
        tok = lambda w: pl.BlockSpec((1, tm, w), lambda b, i: (b, i, 0))
        mod = pl.BlockSpec((1, 1, d), lambda b, i: (b, 0, 0))
        oshape = lambda w, dt: jax.ShapeDtypeStruct((b_, t_, w), dt)
        h2_spec = pl.BlockSpec((1, tm * ROW_TILES, LANES), lambda b, i: (b, i, 0))
        h2_shape = jax.ShapeDtypeStruct((b_, t_ * ROW_TILES, LANES), F32)
    else:
        n, d = x.shape
        tm = n
        grid = (1, 1)
        tok = lambda w: pl.BlockSpec((n, w), lambda b, i: (0, 0))
        mod = tok(d)
        oshape = lambda w, dt: jax.ShapeDtypeStruct((n, w), dt)
        h2_spec = pl.BlockSpec((n * ROW_TILES, LANES), lambda b, i: (0, 0))
        h2_shape = jax.ShapeDtypeStruct((n * ROW_TILES, LANES), F32)
    const2 = lambda b, i: (0, 0)
    return pl.pallas_call(
        functools.partial(_outproj_kernel, batched=batched),
        grid=grid,
        in_specs=[
            tok(d), tok(CONV_W), tok(ATT_W), tok(CHUNK_W), mod, mod, mod,
            pl.BlockSpec((d, d), const2),
            pl.BlockSpec((1, d), const2),
            pl.BlockSpec((1, d), const2),
            pl.BlockSpec((d, LANES), const2),
            pl.BlockSpec((1, LANES), const2),
            pl.BlockSpec((SUBLANES, LANES), const2),
        ],
        out_specs=[tok(d), h2_spec, tok(LANES), tok(LANES), pl.BlockSpec((SUBLANES, LANES), const2)],
        out_shape=[oshape(d, F32), h2_shape, oshape(LANES, jnp.int32), oshape(LANES, F32),
                   jax.ShapeDtypeStruct((SUBLANES, LANES), F32)],
        scratch_shapes=[pltpu.VMEM((SUBLANES, LANES), F32)],
        compiler_params=_cparams(("arbitrary", "arbitrary")),
        name="outproj_prompt" if batched else "outproj_sample",
    )(x, co, att, cho, g1, sc2, sh2, w_out_b, ln_g, ln_b, wr_pad, br_pad, carry_in)


def _dispatch_kernel(dest_ref, h_ref, xs_in_ref, xs_ref, sem):
    del xs_in_ref
    td = h_ref.shape[0] // ROW_TILES

    def row_copy(r, dst):
        return pltpu.make_async_copy(h_ref.at[pl.ds(pl.multiple_of(r * ROW_TILES, ROW_TILES), ROW_TILES)],
                                     xs_ref.at[pl.ds(pl.multiple_of(dst * ROW_TILES, ROW_TILES), ROW_TILES)], sem)

    def issue(r, carry):
        for k in range(TOP_K):
            row_copy(r, dest_ref[0, 0, r * TOP_K + k]).start(priority=k % 2)
        return carry

    lax.fori_loop(0, td, issue, 0, unroll=4)

    def drain(i, carry):
        row_copy(0, 0).wait()
        return carry

    lax.fori_loop(0, td * TOP_K, drain, 0, unroll=8)


def _dispatch(dest, h3d, xs):
    n = h3d.shape[0] // ROW_TILES
    td = min(TD_ROWS, n)
    nt = n // td
    return pl.pallas_call(
        _dispatch_kernel,
        grid=(nt,),
        in_specs=[
            pl.BlockSpec((1, 1, td * TOP_K), lambda i: (i, 0, 0), memory_space=pltpu.SMEM),
            pl.BlockSpec((td * ROW_TILES, LANES), lambda i: (i, 0)),
            pl.BlockSpec(memory_space=pl.ANY),
        ],
        out_specs=pl.BlockSpec(memory_space=pl.ANY),
        scratch_shapes=[pltpu.SemaphoreType.DMA(())],
        out_shape=jax.ShapeDtypeStruct(xs.shape, xs.dtype),
        input_output_aliases={2: 0},
        compiler_params=_cparams(("arbitrary",)),
        name="moe_dispatch",
    )(dest.reshape(nt, 1, td * TOP_K), h3d, xs)


def _expert_kernel(l_ref, be_ref, nu_ref, x_ref, wg_ref, bg_ref, wu_ref, bu_ref, wd_ref, bd_ref,
                   y_ref, wgb_ref, wub_ref, wdb_ref):
    b = pl.program_id(0)
    changed = (b == 0) | (be_ref[b] != be_ref[jnp.maximum(b - 1, 0)])

    @pl.when(changed)
    def _():
        wgb_ref[...] = wg_ref[0, 0].astype(BF16)
        wub_ref[...] = wu_ref[0, 0].astype(BF16)
        wdb_ref[...] = wd_ref[0, 0].astype(BF16)

    @pl.when(b < nu_ref[0])
    def _():
        x = _load_rows_from_tiles(x_ref).astype(BF16)
        g = jnp.minimum(_dot(x, wgb_ref[...]) + bg_ref[0, 0], SWIGLU_LIMIT)
        u = jnp.clip(_dot(x, wub_ref[...]) + bu_ref[0, 0], -SWIGLU_LIMIT, SWIGLU_LIMIT)
        hmid = (u + 1.0) * (g * jax.nn.sigmoid(g * SWIGLU_ALPHA))
        _store_rows_as_tiles(y_ref, _dot(hmid.astype(BF16), wdb_ref[...]) + bd_ref[0, 0])

    @pl.when(b >= nu_ref[0])
    def _():
        y_ref[...] = jnp.zeros(y_ref.shape, y_ref.dtype)


def _experts(layer, block_e, n_used, xs, w_gate, b_gate, w_up, b_up, w_down, b_down):
    n_rows = xs.shape[0] // ROW_TILES
    d = w_gate.shape[-2]
    bm = BM_EXPERT
    nb = n_rows // bm
    dff = w_gate.shape[-1]
    wmap = lambda b, l, be, nu: (l[0], be[b], 0, 0)
    grid_spec = pltpu.PrefetchScalarGridSpec(
        num_scalar_prefetch=3,
        grid=(nb,),
        in_specs=[
            pl.BlockSpec((bm * ROW_TILES, LANES), lambda b, l, be, nu: (b, 0)),
            pl.BlockSpec((1, 1, d, dff), wmap),
            pl.BlockSpec((1, 1, 1, dff), wmap),
            pl.BlockSpec((1, 1, d, dff), wmap),
            pl.BlockSpec((1, 1, 1, dff), wmap),
            pl.BlockSpec((1, 1, dff, d), wmap),
            pl.BlockSpec((1, 1, 1, d), wmap),
        ],
        out_specs=pl.BlockSpec((bm * ROW_TILES, LANES), lambda b, l, be, nu: (b, 0)),
        scratch_shapes=[pltpu.VMEM((d, dff), BF16), pltpu.VMEM((d, dff), BF16), pltpu.VMEM((dff, d), BF16)],
    )
    depth, ne = b_gate.shape[:2]
    return pl.pallas_call(
        _expert_kernel,
        grid_spec=grid_spec,
        out_shape=jax.ShapeDtypeStruct((n_rows * ROW_TILES, LANES), F32),
        compiler_params=_cparams(("arbitrary",)),
        name="moe_experts",
    )(layer, block_e, n_used, xs, w_gate, b_gate.reshape(depth, ne, 1, dff), w_up,
      b_up.reshape(depth, ne, 1, dff), w_down, b_down.reshape(depth, ne, 1, d))


def _combine_kernel(dest_ref, gate_ref, x1_ref, g2_ref, lg_ref, lb_ref, yb_ref, o_ref, buf_ref, sem):
    tc = x1_ref.shape[0]

    def row_copy(src, k, r):
        return pltpu.make_async_copy(yb_ref.at[pl.ds(pl.multiple_of(src * ROW_TILES, ROW_TILES), ROW_TILES)],
                                     buf_ref.at[k, pl.ds(pl.multiple_of(r * ROW_TILES, ROW_TILES), ROW_TILES)], sem)

    def issue(r, carry):
        for k in range(TOP_K):
            row_copy(dest_ref[0, 0, r * TOP_K + k], k, r).start(priority=k % 2)
        return carry

    lax.fori_loop(0, tc, issue, 0, unroll=4)

    def drain(i, carry):
        row_copy(0, 0, 0).wait()
        return carry

    lax.fori_loop(0, tc * TOP_K, drain, 0, unroll=8)

    gates = gate_ref[...]
    f = gates[:, 0:1] * _load_rows_from_tiles(buf_ref.at[0])
    for k in range(1, TOP_K):
        f = f + gates[:, k:k + 1] * _load_rows_from_tiles(buf_ref.at[k])
    o_ref[...] = _layer_norm_rows(DN_ALPHA * x1_ref[...] + (1.0 + g2_ref[...]) * f,
                                  lg_ref[...], lb_ref[...])


def _combine(dest, gates, x1_2d, g2, ln_g, ln_b, yb, rows_per_mod):
    n, d = x1_2d.shape
    tc = min(TD_ROWS, n)
    nt = n // tc
    if rows_per_mod == 1:
        g2_spec = pl.BlockSpec((tc, d), lambda i: (i, 0))
        g2_in = g2
    else:
        tiles_per_mod = rows_per_mod // tc
        g2_spec = pl.BlockSpec((1, 1, d), lambda i: (i // tiles_per_mod, 0, 0))
        g2_in = g2.reshape(g2.shape[0], 1, d)
    kern = _combine_kernel if rows_per_mod == 1 else _combine_kernel_bcast
    dest_t = dest.reshape(nt, 1, tc * TOP_K)
    return pl.pallas_call(
        kern,
        grid=(nt,),
        in_specs=[
            pl.BlockSpec((1, 1, tc * TOP_K), lambda i: (i, 0, 0), memory_space=pltpu.SMEM),
            pl.BlockSpec((tc, LANES), lambda i: (i, 0)),
            pl.BlockSpec((tc, d), lambda i: (i, 0)),
            g2_spec,
            pl.BlockSpec((1, d), lambda i: (0, 0)),
            pl.BlockSpec((1, d), lambda i: (0, 0)),
            pl.BlockSpec(memory_space=pl.ANY),
        ],
        out_specs=pl.BlockSpec((tc, d), lambda i: (i, 0)),
        scratch_shapes=[pltpu.VMEM((TOP_K, tc * ROW_TILES, LANES), F32), pltpu.SemaphoreType.DMA(())],
        out_shape=jax.ShapeDtypeStruct((n, d), F32),
        compiler_params=_cparams(("arbitrary",)),
        name="moe_combine",
    )(dest_t, gates, x1_2d, g2_in, ln_g, ln_b, yb)


def _combine_kernel_bcast(dest_ref, gate_ref, x1_ref, g2_ref, lg_ref, lb_ref, yb_ref, o_ref, buf_ref, sem):
    _combine_kernel(dest_ref, gate_ref, x1_ref, g2_ref.at[0], lg_ref, lb_ref, yb_ref, o_ref, buf_ref, sem)


def _rope_tables(pos):
    half = QK_DIM // 2
    inv = ROPE_THETA ** (-jnp.arange(half, dtype=F32) / half)
    ang = pos.astype(F32)[:, None] * inv[None, :]
    cos = jnp.cos(ang)
    sin = jnp.sin(ang)
    reps = LANES // QK_DIM
    cos_t = jnp.tile(jnp.concatenate([cos, cos], axis=1), (1, reps))
    sin_t = jnp.tile(jnp.concatenate([-sin, sin], axis=1), (1, reps))
    return cos_t, sin_t


def _pad_rows(a, rows):
    return jnp.concatenate([a, jnp.zeros((rows - a.shape[0],) + a.shape[1:], a.dtype)], axis=0)


def kernel(x_prompt, x_sample, c_prompt, c_sample, cache_k, cache_v, state_conv, page_table, w_ada, b_ada, w_in, conv_w, lambda_q1, lambda_k1, lambda_q2, lambda_k2, subln_g, chunk_ln_g, chunk_ln_b, chunk_ws, chunk_bs, w_out, ln1_g, ln1_b, w_router, b_router, w_gate, b_gate, w_up, b_up, w_down, b_down, ln2_g, ln2_b):
    n_prompt, seq, d = x_prompt.shape
    n_dec, dec_seq, _ = x_sample.shape
    depth = w_in.shape[0]
    n_pool = cache_k.shape[1]
    n_pages = page_table.shape[1]
    past_len = n_pages * PAGE_SIZE
    ns = n_dec * dec_seq
    np_tok = n_prompt * seq
    assert dec_seq >= CONV_K - 1 and dec_seq <= SUBLANES and CHUNK % dec_seq == 0
    assert ns % CHUNK == 0 and seq % CHUNK == 0

    c_all = jnp.concatenate([c_prompt, c_sample], axis=0)
    m_rows = -(-c_all.shape[0] // SUBLANES) * SUBLANES
    mod_all = _ada_call(_pad_rows(c_all, m_rows), w_ada, b_ada)

    cos_p, sin_p = _rope_tables(jnp.arange(seq))
    cos_s, sin_s = _rope_tables(jnp.tile(past_len + jnp.arange(dec_seq), n_dec))

    cache_k4 = jnp.transpose(cache_k, (0, 1, 3, 4, 5, 2)).reshape(depth, n_pool, ATT_HEADS * 2 * QK_DIM, PAGE_SIZE)
    cache_v4 = jnp.transpose(cache_v, (0, 1, 3, 4, 2)).reshape(depth, n_pool, ATT_W, PAGE_SIZE)
    conv0 = jnp.zeros((n_prompt, CONV_K - 1, CONV_W), F32)
    lane = jnp.arange(LANES)
    tok_pos = jnp.arange(ns) % dec_seq

    nrow = 2 * dec_seq * ATT_HEADS
    r_map = jnp.arange(nrow) // (dec_seq * ATT_HEADS)
    r_q = (jnp.arange(nrow) // ATT_HEADS) % dec_seq
    r_h = jnp.arange(nrow) % ATT_HEADS
    col = jnp.arange(ATT_HEADS * 2 * QK_DIM)
    qbd_mask = ((col[None, :] // (2 * QK_DIM)) == r_h[:, None]) & (((col[None, :] // QK_DIM) % 2) == r_map[:, None])

    n_assign = (np_tok + ns) * TOP_K
    n_blocks = -(-n_assign // BM_EXPERT) + N_EXPERTS
    n_rows = n_blocks * BM_EXPERT

    y_p, y_s = x_prompt, x_sample.reshape(ns, d)
    xs = jnp.zeros((n_rows * ROW_TILES, LANES), F32)
    outs = {k: [] for k in ("kp", "vp", "cp", "chp", "ks", "vs", "cs", "chs")}
    for l in range(depth):
        layer = jnp.full((1,), l, jnp.int32)
        lam_init = 0.8 - 0.6 * math.exp(-0.3 * l)
        mod = mod_all[l]
        mp = mod[:n_prompt].reshape(n_prompt, 6, 1, d)
        ms = jnp.repeat(mod[n_prompt:n_prompt + n_dec], dec_seq, axis=0).reshape(ns, 6, d)
        w_in_b = w_in[l].astype(BF16)
        w_out_b = w_out[l].astype(BF16)
        cw = _pad_rows(conv_w[l], SUBLANES)
        lamp = jnp.zeros((SUBLANES, LANES), F32)
        lamp = lamp.at[0, :QK_DIM].set(lambda_q1[l]).at[1, :QK_DIM].set(lambda_k1[l])
        lamp = lamp.at[2, :QK_DIM].set(lambda_q2[l]).at[3, :QK_DIM].set(lambda_k2[l])
        lamp = lamp.at[4, :].set(lam_init)
        lng = chunk_ln_g[l].reshape(1, CHUNK_W)
        lnb = chunk_ln_b[l].reshape(1, CHUNK_W)
        ws = chunk_ws[l]
        bs = chunk_bs[l]
        wcat_p = jnp.concatenate([jnp.tril(ws[g]) for g in range(CHUNK_GROUPS)], axis=1).astype(BF16)
        bsf_p = jnp.repeat(bs.T, CHUNK_GW, axis=1)
        eye = jnp.eye(CHUNK // dec_seq, dtype=F32)
        wcat_s = jnp.concatenate([jnp.kron(eye, jnp.tril(ws[g, :dec_seq, :dec_seq]))
                                  for g in range(CHUNK_GROUPS)], axis=1).astype(BF16)
        bsf_s = jnp.tile(jnp.repeat(bs[:, :dec_seq].T, CHUNK_GW, axis=1), (CHUNK // dec_seq, 1))
        g_sub = subln_g[l]
        wr_pad = jnp.concatenate([w_router[l], jnp.zeros((d, LANES - N_EXPERTS), F32)], axis=1)
        br_pad = jnp.concatenate([b_router[l], jnp.full((LANES - N_EXPERTS,), NEG_BIG, F32)]).reshape(1, LANES)
        l1g, l1b = ln1_g[l].reshape(1, d), ln1_b[l].reshape(1, d)
        l2g, l2b = ln2_g[l].reshape(1, d), ln2_b[l].reshape(1, d)

        (q_p, kf_p, kb_p, vf_p, vb_p, co_p, cho_p, cst_p, chst_p) = _inproj_prompt(
            y_p, mp[:, 1], mp[:, 0], w_in_b, cos_p, sin_p, cw, conv0, lng, lnb, wcat_p, bsf_p)
        att_p = _attn_prompt(q_p, kb_p, vb_p, lamp, jnp.broadcast_to(g_sub[:, None], (V_DIM, min(TQ_ATT, seq))))
        zero_carry = jnp.zeros((SUBLANES, LANES), F32)
        x1_p, h2_p, route_p, gate_p, cnt_p = _outproj(
            y_p, co_p, att_p, cho_p, mp[:, 2], mp[:, 4], mp[:, 3], w_out_b, l1g, l1b, wr_pad, br_pad,
            zero_carry, True)

        st = state_conv[l]
        prev1 = jnp.repeat(st[:, 1], dec_seq, axis=0)
        prev2 = jnp.where((tok_pos == 0)[:, None], jnp.repeat(st[:, 0], dec_seq, axis=0), prev1)
        (q_s, kf_s, kb_s, vf_s, vb_s, co_s, cho_s, z_s, vn_s) = _inproj_sample(
            y_s, ms[:, 1], ms[:, 0], w_in_b, cos_s, sin_s, cw, prev1, prev2, lng, lnb, wcat_s, bsf_s, dec_seq)
        q3 = q_s.reshape(n_dec, dec_seq, -1)
        qbd = jnp.where(qbd_mask[None], q3[:, r_q, :], jnp.zeros((), BF16))
        pad3 = lambda a: jnp.concatenate(
            [a.reshape(n_dec, dec_seq, -1),
             jnp.zeros((n_dec, NEW_PAD - dec_seq, a.shape[-1]), a.dtype)], axis=1)
        att_s = _attn_sample(layer, page_table, cache_k4, cache_v4, qbd, pad3(kf_s), pad3(vf_s), lamp,
                             jnp.tile(g_sub, ATT_HEADS).reshape(1, ATT_W), dec_seq)
        x1_s, h2_s, route_s, gate_s, cnt_all = _outproj(
            y_s, co_s, att_s.reshape(ns, ATT_W).astype(BF16), cho_s, ms[:, 2], ms[:, 4], ms[:, 3],
            w_out_b, l1g, l1b, wr_pad, br_pad, cnt_p, False)

        counts = cnt_all[0, :N_EXPERTS].astype(jnp.int32)
        pcounts = ((counts + BM_EXPERT - 1) // BM_EXPERT) * BM_EXPERT
        pends = jnp.cumsum(pcounts)
        pstart = (pends - pcounts).astype(jnp.int32)
        blk_row = jnp.arange(n_blocks, dtype=jnp.int32) * BM_EXPERT
        block_e = jnp.minimum(jnp.sum(blk_row[:, None] >= pends[None, :], axis=1), N_EXPERTS - 1).astype(jnp.int32)
        n_used = (pends[-1] // BM_EXPERT).astype(jnp.int32).reshape(1)
        def dest_rows(route):
            idx, rank = route[:, :TOP_K], route[:, TOP_K:2 * TOP_K]
            hot = idx[:, :, None] == jnp.arange(N_EXPERTS, dtype=jnp.int32)[None, None, :]
            return rank + jnp.sum(jnp.where(hot, pstart[None, None, :], 0), axis=-1)

        dest_p = dest_rows(route_p.reshape(np_tok, LANES))
        dest_s = dest_rows(route_s)
        xs = _dispatch(dest_p, h2_p.reshape(np_tok * ROW_TILES, LANES), xs)
        xs = _dispatch(dest_s, h2_s, xs)
        yb = _experts(layer, block_e, n_used, xs, w_gate, b_gate, w_up, b_up, w_down, b_down)
        y_p = _combine(dest_p, gate_p.reshape(np_tok, LANES), x1_p.reshape(np_tok, d),
                       mod[:n_prompt, 5 * d:], l2g, l2b, yb, seq).reshape(n_prompt, seq, d)
        y_s = _combine(dest_s, gate_s, x1_s, ms[:, 5], l2g, l2b, yb, 1)

        outs["kp"].append(jnp.transpose(kf_p.reshape(n_prompt, ATT_HEADS, 2, QK_DIM, seq), (0, 4, 1, 2, 3)))
        outs["vp"].append(jnp.transpose(vf_p.reshape(n_prompt, ATT_HEADS, V_DIM, seq), (0, 3, 1, 2)))
        outs["cp"].append(cst_p)
        outs["chp"].append(chst_p)
        outs["ks"].append(kf_s.reshape(n_dec, dec_seq, ATT_HEADS, 2, QK_DIM))
        outs["vs"].append(vf_s.reshape(n_dec, dec_seq, ATT_HEADS, V_DIM))
        outs["cs"].append(z_s.reshape(n_dec, dec_seq, CONV_W)[:, dec_seq - (CONV_K - 1):])
        outs["chs"].append(vn_s.reshape(n_dec, dec_seq, CHUNK_W))

    st_ = lambda k: jnp.stack(outs[k])
    return (y_p, y_s.reshape(n_dec, dec_seq, d), st_("kp"), st_("vp"), st_("cp"), st_("chp"),
            st_("ks"), st_("vs"), st_("cs"), st_("chs"))
```
